```python
import jax, jax.numpy as jnp
from jax import lax
import numpy as np

D_MODEL = 2048
BATCH = 1
SEQ = 8192
DEPTH = 2
DEC_BATCH = 128
DEC_SEQ = 4
PAST_LEN = 2048
PAGE_SIZE = 128

POOL_WINDOWS = (2, 4, 8, 16)
N_POOL_GROUPS = len(POOL_WINDOWS)
C_POOL = D_MODEL // 2
POOL_GROUP = C_POOL // N_POOL_GROUPS
POOL_STATE = max(POOL_WINDOWS) - 1
C_CONV = D_MODEL // 2
CONV_WIDTH = 31
CONV_STATE = CONV_WIDTH - 1
HEAD_DIM = 128
ATTN_GROUPS = ((128, 1), (512, 4), (2048, 16))
N_GROUPS = len(ATTN_GROUPS)
HEADS_PER_GROUP = 4
N_HEADS = N_GROUPS * HEADS_PER_GROUP
C_ATTN = N_HEADS * HEAD_DIM
C_ATTN_OUT = HEADS_PER_GROUP * HEAD_DIM
ROT_DIM = HEAD_DIM // 4
ROPE_THETA = 500000.0
D_FF = 5632
N_BRANCH = 3
QBLK = 128
EPS = 1e-6
IN_SPLITS = (C_POOL, C_CONV, C_CONV, C_ATTN, C_ATTN, C_ATTN, N_BRANCH * D_MODEL)
IN_COLS = sum(IN_SPLITS)

kernel_name = 'hybrid_pool_conv_dilated_attn_decoder_step'


def _rmsnorm(x, g):
    xf = x.astype(jnp.float32)
    y = xf * lax.rsqrt(jnp.mean(xf * xf, axis=-1, keepdims=True) + EPS)
    return (y * g.astype(jnp.float32)).astype(x.dtype)


def _swiglu(x, wg, wu, wd):
    return (jax.nn.silu(x @ wg) * (x @ wu)) @ wd


def _rope(x, pos):
    half = ROT_DIM // 2
    inv = ROPE_THETA ** (-jnp.arange(half, dtype=jnp.float32) / half)
    ang = pos.astype(jnp.float32)[:, None] * inv[None, :]
    cos = jnp.cos(ang)[:, None, :]
    sin = jnp.sin(ang)[:, None, :]
    xr = x[..., :ROT_DIM].astype(jnp.float32)
    x1, x2 = xr[..., :half], xr[..., half:]
    rot = jnp.concatenate([x1 * cos - x2 * sin, x2 * cos + x1 * sin], axis=-1)
    return jnp.concatenate([rot.astype(x.dtype), x[..., ROT_DIM:]], axis=-1)


def _pool_branch(u, prev, pos0, pool_w, pool_scale):
    T = u.shape[1]
    P = prev.shape[1]
    ext = jnp.concatenate([prev.astype(u.dtype), u], axis=1)
    cs = jnp.pad(jnp.cumsum(ext.astype(jnp.float32), axis=1), ((0, 0), (1, 0), (0, 0)))
    pos = pos0 + jnp.arange(T)
    uf = u.astype(jnp.float32)
    diffs = []
    for g, w in enumerate(POOL_WINDOWS):
        c0, c1 = g * POOL_GROUP, (g + 1) * POOL_GROUP
        win_sum = cs[:, P + 1:P + 1 + T, c0:c1] - cs[:, P + 1 - w:P + 1 - w + T, c0:c1]
        cnt = jnp.minimum(pos + 1, w).astype(jnp.float32)[None, :, None]
        diffs.append(win_sum / cnt - uf[:, :, c0:c1])
    d = jnp.stack(diffs, axis=2).astype(u.dtype)
    y = jnp.einsum('btgc,gce->btge', d, pool_w).reshape(u.shape)
    return y * pool_scale, ext[:, -POOL_STATE:]


def _conv_branch(a, prev, conv_w, conv_b, ln_g, ln_b):
    ext = jnp.concatenate([prev.astype(a.dtype), a], axis=1)
    c = lax.conv_general_dilated(ext, conv_w[:, None, :].astype(a.dtype), (1,), 'VALID',
                                 dimension_numbers=('NWC', 'WIO', 'NWC'),
                                 feature_group_count=C_CONV)
    cf = c.astype(jnp.float32) + conv_b.astype(jnp.float32)
    mu = jnp.mean(cf, axis=-1, keepdims=True)
    var = jnp.mean(jnp.square(cf - mu), axis=-1, keepdims=True)
    cn = (cf - mu) * lax.rsqrt(var + EPS) * ln_g.astype(jnp.float32) + ln_b.astype(jnp.float32)
    return jax.nn.silu(cn).astype(a.dtype), ext[:, -CONV_STATE:]


def _heads(t, g):
    return t[:, :, g * HEADS_PER_GROUP:(g + 1) * HEADS_PER_GROUP]


def _dilated_attend(q, kb, vb, rel, valid):
    kg = jnp.take(kb, rel, axis=1)
    vg = jnp.take(vb, rel, axis=1)
    s = jnp.einsum('bqhd,bqkhd->bqhk', q, kg, preferred_element_type=jnp.float32) * (HEAD_DIM ** -0.5)
    s = jnp.where(valid[None, :, None, :], s, -jnp.inf)
    m = jnp.max(s, axis=-1, keepdims=True)
    e = jnp.exp(s - m)
    den = jnp.sum(e, axis=-1, keepdims=True)
    o = jnp.einsum('bqhk,bqkhd->bqhd', (e / den).astype(vb.dtype), vg)
    lse = (m + jnp.log(den))[..., 0]
    return o, lse


def _merge_groups(outs, lses):
    a = jax.nn.softmax(jnp.stack(lses, axis=0), axis=0)
    o = jnp.sum(jnp.stack(outs, axis=0).astype(jnp.float32) * a[..., None], axis=0)
    return o.astype(outs[0].dtype)


def _attn_prompt(q, k, v):
    B, S = q.shape[:2]
    nb = S // QBLK
    kps, vps = [], []
    for g, (w, d) in enumerate(ATTN_GROUPS):
        kps.append(jnp.pad(_heads(k, g), ((0, 0), (w, 0), (0, 0), (0, 0))))
        vps.append(jnp.pad(_heads(v, g), ((0, 0), (w, 0), (0, 0), (0, 0))))
    qb = jnp.moveaxis(q.reshape(B, nb, QBLK, N_HEADS, HEAD_DIM), 1, 0)
    t0s = jnp.arange(nb, dtype=jnp.int32) * QBLK
    j = np.arange(QBLK)[:, None]

    def block(args):
        qblk, t0 = args
        outs, lses = [], []
        for g, (w, d) in enumerate(ATTN_GROUPS):
            dist = d * np.arange(w // d + 1)[None, :]
            rel = w + j - dist
            kb = lax.dynamic_slice_in_dim(kps[g], t0, w + QBLK, axis=1)
            vb = lax.dynamic_slice_in_dim(vps[g], t0, w + QBLK, axis=1)
            valid = (t0 + jnp.asarray(j - dist)) >= 0
            o, lse = _dilated_attend(_heads(qblk, g), kb, vb, rel, valid)
            outs.append(o)
            lses.append(lse)
        return _merge_groups(outs, lses)

    o = lax.map(block, (qb, t0s))
    return jnp.moveaxis(o, 0, 1).reshape(B, S, C_ATTN_OUT)


def _attn_sample(q, k, v, kv_prev):
    DB, T = q.shape[:2]
    i = np.arange(T)[:, None]
    outs, lses = [], []
    for g, (w, d) in enumerate(ATTN_GROUPS):
        ck, cv = kv_prev[g]
        L = ck.shape[1]
        kb = jnp.concatenate([ck.astype(k.dtype), _heads(k, g)], axis=1)
        vb = jnp.concatenate([cv.astype(v.dtype), _heads(v, g)], axis=1)
        rel = L + i - d * np.arange(w // d + 1)[None, :]
        valid = jnp.asarray(rel >= 0)
        o, lse = _dilated_attend(_heads(q, g), kb, vb, np.maximum(rel, 0), valid)
        outs.append(o)
        lses.append(lse)
    return _merge_groups(outs, lses).reshape(DB, T, C_ATTN_OUT)


def _layer(x, pos0, pool_prev, conv_prev, kv_prev, ffn1_norm, ffn1_wg, ffn1_wu, ffn1_wd, mix_norm,
           w_in, b_gate, pool_w, pool_scale, conv_w, conv_b, conv_ln_g, conv_ln_b, q_norm, k_norm,
           proj_pool, proj_conv, proj_attn, w_out, ffn2_norm, ffn2_wg, ffn2_wu, ffn2_wd):
    B, T, _ = x.shape
    x = x + 0.5 * _swiglu(_rmsnorm(x, ffn1_norm), ffn1_wg, ffn1_wu, ffn1_wd)
    h = _rmsnorm(x, mix_norm)
    z = h @ w_in
    zp, za, zb, zq, zk, zv, zg = jnp.split(z, np.cumsum(IN_SPLITS)[:-1].tolist(), axis=-1)
    pool_out, pool_new = _pool_branch(zp, pool_prev, pos0, pool_w, pool_scale)
    conv_out, conv_new = _conv_branch(za * jax.nn.sigmoid(zb), conv_prev, conv_w, conv_b,
                                      conv_ln_g, conv_ln_b)
    pos = pos0 + jnp.arange(T)
    q = _rope(_rmsnorm(zq.reshape(B, T, N_HEADS, HEAD_DIM), q_norm), pos)
    k = _rope(_rmsnorm(zk.reshape(B, T, N_HEADS, HEAD_DIM), k_norm), pos)
    v = zv.reshape(B, T, N_HEADS, HEAD_DIM)
    if kv_prev is None:
        attn_out = _attn_prompt(q, k, v)
        kv_new = tuple((_heads(k, g)[:, -min(w, T):], _heads(v, g)[:, -min(w, T):])
                       for g, (w, d) in enumerate(ATTN_GROUPS))
    else:
        attn_out = _attn_sample(q, k, v, kv_prev)
        kv_new = tuple((_heads(k, g), _heads(v, g)) for g in range(N_GROUPS))
    gates = jax.nn.sigmoid(zg + b_gate).reshape(B, T, N_BRANCH, D_MODEL)
    merged = (gates[:, :, 0] * (pool_out @ proj_pool)
              + gates[:, :, 1] * (conv_out @ proj_conv)
              + gates[:, :, 2] * (attn_out @ proj_attn))
    x = x + merged @ w_out
    x = x + 0.5 * _swiglu(_rmsnorm(x, ffn2_norm), ffn2_wg, ffn2_wu, ffn2_wd)
    return x, pool_new, conv_new, kv_new


def setup_inputs(seed: int = 0) -> dict:
    key = jax.random.key(seed)
    ks = iter(jax.random.split(key, 48))

    def nrm(shape, scale):
        return jax.random.normal(next(ks), shape, jnp.float32) * scale

    def gain(shape, s=0.05):
        return 1.0 + s * jax.random.normal(next(ks), shape, jnp.float32)

    lens = [min(w, PAST_LEN) for (w, d) in ATTN_GROUPS]
    cshape = lambda L: (DEPTH, DEC_BATCH, L, HEADS_PER_GROUP, HEAD_DIM)
    return {
        'x_prompt': nrm((BATCH, SEQ, D_MODEL), 1.0),
        'x_sample': nrm((DEC_BATCH, DEC_SEQ, D_MODEL), 1.0),
        'cache_k_d1': nrm(cshape(lens[0]), 1.0),
        'cache_v_d1': nrm(cshape(lens[0]), 1.0),
        'cache_k_d4': nrm(cshape(lens[1]), 1.0),
        'cache_v_d4': nrm(cshape(lens[1]), 1.0),
        'cache_k_d16': nrm(cshape(lens[2]), 1.0),
        'cache_v_d16': nrm(cshape(lens[2]), 1.0),
        'state_pool': nrm((DEPTH, DEC_BATCH, POOL_STATE, C_POOL), 1.0),
        'state_conv': nrm((DEPTH, DEC_BATCH, CONV_STATE, C_CONV), 0.5),
        'ffn1_norm': gain((DEPTH, D_MODEL)),
        'ffn1_wg': nrm((DEPTH, D_MODEL, D_FF), D_MODEL ** -0.5),
        'ffn1_wu': nrm((DEPTH, D_MODEL, D_FF), D_MODEL ** -0.5),
        'ffn1_wd': nrm((DEPTH, D_FF, D_MODEL), D_FF ** -0.5),
        'mix_norm': gain((DEPTH, D_MODEL)),
        'w_in': nrm((DEPTH, D_MODEL, IN_COLS), D_MODEL ** -0.5),
        'b_gate': nrm((DEPTH, N_BRANCH * D_MODEL), 0.1),
        'pool_w': nrm((DEPTH, N_POOL_GROUPS, POOL_GROUP, POOL_GROUP), POOL_GROUP ** -0.5),
        'pool_scale': gain((DEPTH, C_POOL), 0.1),
        'conv_w': nrm((DEPTH, CONV_WIDTH, C_CONV), CONV_WIDTH ** -0.5),
        'conv_b': nrm((DEPTH, C_CONV), 0.02),
        'conv_ln_g': gain((DEPTH, C_CONV)),
        'conv_ln_b': nrm((DEPTH, C_CONV), 0.02),
        'q_norm': gain((DEPTH, HEAD_DIM)),
        'k_norm': gain((DEPTH, HEAD_DIM)),
        'proj_pool': nrm((DEPTH, C_POOL, D_MODEL), C_POOL ** -0.5),
        'proj_conv': nrm((DEPTH, C_CONV, D_MODEL), C_CONV ** -0.5),
        'proj_attn': nrm((DEPTH, C_ATTN_OUT, D_MODEL), C_ATTN_OUT ** -0.5),
        'w_out': nrm((DEPTH, D_MODEL, D_MODEL), D_MODEL ** -0.5),
        'ffn2_norm': gain((DEPTH, D_MODEL)),
        'ffn2_wg': nrm((DEPTH, D_MODEL, D_FF), D_MODEL ** -0.5),
        'ffn2_wu': nrm((DEPTH, D_MODEL, D_FF), D_MODEL ** -0.5),
        'ffn2_wd': nrm((DEPTH, D_FF, D_MODEL), D_FF ** -0.5),
    }


def reference(x_prompt, x_sample, cache_k_d1, cache_v_d1, cache_k_d4, cache_v_d4, cache_k_d16,
              cache_v_d16, state_pool, state_conv, ffn1_norm, ffn1_wg, ffn1_wu, ffn1_wd, mix_norm,
              w_in, b_gate, pool_w, pool_scale, conv_w, conv_b, conv_ln_g, conv_ln_b, q_norm, k_norm,
              proj_pool, proj_conv, proj_attn, w_out, ffn2_norm, ffn2_wg, ffn2_wu, ffn2_wd):
    caches_k = (cache_k_d1, cache_k_d4, cache_k_d16)
    caches_v = (cache_v_d1, cache_v_d4, cache_v_d16)
    xp, xs = x_prompt, x_sample
    zero_pool = jnp.zeros((x_prompt.shape[0], POOL_STATE, C_POOL), x_prompt.dtype)
    zero_conv = jnp.zeros((x_prompt.shape[0], CONV_STATE, C_CONV), x_prompt.dtype)
    p_pool, p_conv, s_pool, s_conv = [], [], [], []
    p_k = [[] for _ in range(N_GROUPS)]
    p_v = [[] for _ in range(N_GROUPS)]
    s_k = [[] for _ in range(N_GROUPS)]
    s_v = [[] for _ in range(N_GROUPS)]
    for l in range(DEPTH):
        lw = dict(ffn1_norm=ffn1_norm[l], ffn1_wg=ffn1_wg[l], ffn1_wu=ffn1_wu[l], ffn1_wd=ffn1_wd[l],
                  mix_norm=mix_norm[l], w_in=w_in[l], b_gate=b_gate[l], pool_w=pool_w[l],
                  pool_scale=pool_scale[l], conv_w=conv_w[l], conv_b=conv_b[l],
                  conv_ln_g=conv_ln_g[l], conv_ln_b=conv_ln_b[l], q_norm=q_norm[l], k_norm=k_norm[l],
                  proj_pool=proj_pool[l], proj_conv=proj_conv[l], proj_attn=proj_attn[l],
                  w_out=w_out[l], ffn2_norm=ffn2_norm[l], ffn2_wg=ffn2_wg[l], ffn2_wu=ffn2_wu[l],
                  ffn2_wd=ffn2_wd[l])
        xp, pn, cn, kvn = _layer(xp, 0, zero_pool, zero_conv, None, **lw)
        p_pool.append(pn)
        p_conv.append(cn)
        for g in range(N_GROUPS):
            p_k[g].append(kvn[g][0])
            p_v[g].append(kvn[g][1])
        kv_prev = tuple((caches_k[g][l], caches_v[g][l]) for g in range(N_GROUPS))
        xs, pn, cn, kvn = _layer(xs, PAST_LEN, state_pool[l], state_conv[l], kv_prev, **lw)
        s_pool.append(pn)
        s_conv.append(cn)
        for g in range(N_GROUPS):
            s_k[g].append(kvn[g][0])
            s_v[g].append(kvn[g][1])
    st = lambda lst: jnp.stack(lst, axis=0)
    return (xp, xs,
            st(p_k[0]), st(p_v[0]), st(p_k[1]), st(p_v[1]), st(p_k[2]), st(p_v[2]), st(p_pool), st(p_conv),
            st(s_k[0]), st(s_v[0]), st(s_k[1]), st(s_v[1]), st(s_k[2]), st(s_v[2]), st(s_pool), st(s_conv))
```

```python
import functools

import numpy as np
import jax
import jax.numpy as jnp
from jax import lax
from jax.experimental import pallas as pl
from jax.experimental.pallas import tpu as pltpu

F32 = jnp.float32
BF16 = jnp.bfloat16

PAST_LEN = 2048
POOL_WINDOWS = (2, 4, 8, 16)
POOL_STATE = max(POOL_WINDOWS) - 1
CONV_WIDTH = 31
CONV_STATE = CONV_WIDTH - 1
HEAD_DIM = 128
ATTN_GROUPS = ((128, 1), (512, 4), (2048, 16))
N_GROUPS = len(ATTN_GROUPS)
HEADS_PER_GROUP = 4
N_HEADS = N_GROUPS * HEADS_PER_GROUP
KEYS_PER_QUERY = 128
ROT_DIM = HEAD_DIM // 4
ROPE_THETA = 500000.0
EPS = 1e-6
MASKED = -1e30

TOKEN_TILE = 512
FF_TILE = 512
IN_TILE = 1536
MIX_TILE = 512
SEQ_TILE = 512
POOL_HALO = 16
CONV_HALO = 32
ATT_CHUNK = 512
ATT_BLOCK = 128
SAMPLE_BATCH_TILE = 32
SAMPLE_ATTN_TILE = 4
CONV_ROWS = 64
MIB = 1 << 20


def _params(semantics, vmem_mib):
    return pltpu.CompilerParams(dimension_semantics=semantics, vmem_limit_bytes=vmem_mib * MIB)


def _rms(x, gain):
    return x * lax.rsqrt(jnp.mean(x * x, axis=-1, keepdims=True) + EPS) * gain


def _sigmoid(x):
    return 1.0 / (1.0 + jnp.exp(-x))


def _ffn_kernel(x_ref, gain_ref, wg_ref, wu_ref, wd_ref, o_ref, h_ref):
    @pl.when(pl.program_id(1) == 0)
    def _():
        x = x_ref[...]
        h_ref[...] = _rms(x, gain_ref[...]).astype(BF16)
        o_ref[...] = x

    h = h_ref[...]
    g = jnp.dot(h, wg_ref[...], preferred_element_type=F32)
    u = jnp.dot(h, wu_ref[...], preferred_element_type=F32)
    a = (g * _sigmoid(g) * u).astype(BF16)
    o_ref[...] += 0.5 * jnp.dot(a, wd_ref[...], preferred_element_type=F32)


def _ffn(x, gain, wg, wu, wd, layer):
    n, dm = x.shape
    dff = wg.shape[-1]
    tm, tf = TOKEN_TILE, FF_TILE
    return pl.pallas_call(
        _ffn_kernel,
        grid=(n // tm, dff // tf),
        in_specs=[
            pl.BlockSpec((tm, dm), lambda i, j: (i, 0)),
            pl.BlockSpec((None, 1, dm), lambda i, j: (layer, 0, 0)),
            pl.BlockSpec((None, dm, tf), lambda i, j: (layer, 0, j)),
            pl.BlockSpec((None, dm, tf), lambda i, j: (layer, 0, j)),
            pl.BlockSpec((None, tf, dm), lambda i, j: (layer, j, 0)),
        ],
        out_specs=pl.BlockSpec((tm, dm), lambda i, j: (i, 0)),
        out_shape=jax.ShapeDtypeStruct((n, dm), F32),
        scratch_shapes=[pltpu.VMEM((tm, dm), BF16)],
        compiler_params=_params(("parallel", "arbitrary"), 48),
        name="ffn",
    )(x, gain, wg, wu, wd)


def _inproj_kernel(x_ref, gain_ref, w_ref, z_ref, h_ref):
    @pl.when(pl.program_id(1) == 0)
    def _():
        h_ref[...] = _rms(x_ref[...], gain_ref[...]).astype(BF16)

    z_ref[...] = jnp.dot(h_ref[...], w_ref[...], preferred_element_type=F32)


def _inproj(x, gain, w, layer):
    n, dm = x.shape
    cols = w.shape[-1]
    tm, tn = TOKEN_TILE, IN_TILE
    return pl.pallas_call(
        _inproj_kernel,
        grid=(n // tm, cols // tn),
        in_specs=[
            pl.BlockSpec((tm, dm), lambda i, j: (i, 0)),
            pl.BlockSpec((None, 1, dm), lambda i, j: (layer, 0, 0)),
            pl.BlockSpec((None, dm, tn), lambda i, j: (layer, 0, j)),
        ],
        out_specs=pl.BlockSpec((tm, tn), lambda i, j: (i, j)),
        out_shape=jax.ShapeDtypeStruct((n, cols), F32),
        scratch_shapes=[pltpu.VMEM((tm, dm), BF16)],
        compiler_params=_params(("parallel", "arbitrary"), 48),
        name="inproj",
    )(x, gain, w)


def _qknorm_kernel(z_ref, gain_ref, cos_ref, sin_ref, o_ref):
    half = ROT_DIM // 2
    c = cos_ref[...]
    s = sin_ref[...]
    lane = lax.broadcasted_iota(jnp.int32, c.shape, 1)
    for hh in range(2 * N_HEADS):
        cols = slice(hh * HEAD_DIM, (hh + 1) * HEAD_DIM)
        y = _rms(z_ref[:, cols], gain_ref[pl.ds(hh // N_HEADS, 1), :])
        partner = jnp.where(lane < half, pltpu.roll(y, HEAD_DIM - half, 1), pltpu.roll(y, half, 1))
        o_ref[:, cols] = y * c + partner * s


def _qknorm(z, gains, cos_t, sin_t, qk_col_block):
    n = z.shape[0]
    width = 2 * N_HEADS * HEAD_DIM
    tm = TOKEN_TILE
    return pl.pallas_call(
        _qknorm_kernel,
        grid=(n // tm,),
        in_specs=[
            pl.BlockSpec((tm, width), lambda i: (i, qk_col_block)),
            pl.BlockSpec((2, HEAD_DIM), lambda i: (0, 0)),
            pl.BlockSpec((tm, HEAD_DIM), lambda i: (i, 0)),
            pl.BlockSpec((tm, HEAD_DIM), lambda i: (i, 0)),
        ],
        out_specs=pl.BlockSpec((tm, width), lambda i: (i, 0)),
        out_shape=jax.ShapeDtypeStruct((n, width), F32),
        compiler_params=_params(("parallel",), 40),
        name="qknorm_rope",
    )(z, gains, cos_t, sin_t)


def _mixout_kernel(x_ref, pool_ref, conv_ref, attn_ref, zg0_ref, zg1_ref, zg2_ref, bg_ref,
                   pp_ref, pc_ref, pa_ref, wo_ref, o_ref):
    @pl.when(pl.program_id(1) == 0)
    def _():
        o_ref[...] = x_ref[...]

    def gate(zg_ref, b):
        return _sigmoid(zg_ref[...] + bg_ref[pl.ds(b, 1), :])

    merged = gate(zg0_ref, 0) * jnp.dot(pool_ref[...], pp_ref[...], preferred_element_type=F32)
    merged += gate(zg1_ref, 1) * jnp.dot(conv_ref[...], pc_ref[...], preferred_element_type=F32)
    merged += gate(zg2_ref, 2) * jnp.dot(attn_ref[...], pa_ref[...], preferred_element_type=F32)
    o_ref[...] += jnp.dot(merged.astype(BF16), wo_ref[...], preferred_element_type=F32)


def _mixout(x, pool_o, conv_o, attn_o, z, gate_col0, b_gate, proj_pool, proj_conv, proj_attn, w_out, layer):
    n, dm = x.shape
    tm, tn = TOKEN_TILE, MIX_TILE
    g0 = gate_col0 // tn
    per = dm // tn

    def zg_spec(b):
        return pl.BlockSpec((tm, tn), lambda i, j: (i, g0 + b * per + j))

    return pl.pallas_call(
        _mixout_kernel,
        grid=(n // tm, per),
        in_specs=[
            pl.BlockSpec((tm, dm), lambda i, j: (i, 0)),
            pl.BlockSpec((tm, pool_o.shape[1]), lambda i, j: (i, 0)),
            pl.BlockSpec((tm, conv_o.shape[1]), lambda i, j: (i, 0)),
            pl.BlockSpec((tm, attn_o.shape[1]), lambda i, j: (i, 0)),
            zg_spec(0), zg_spec(1), zg_spec(2),
            pl.BlockSpec((None, 3, tn), lambda i, j: (layer, 0, j)),
            pl.BlockSpec((None, proj_pool.shape[1], tn), lambda i, j: (layer, 0, j)),
            pl.BlockSpec((None, proj_conv.shape[1], tn), lambda i, j: (layer, 0, j)),
            pl.BlockSpec((None, proj_attn.shape[1], tn), lambda i, j: (layer, 0, j)),
            pl.BlockSpec((None, tn, dm), lambda i, j: (layer, j, 0)),
        ],
        out_specs=pl.BlockSpec((tm, dm), lambda i, j: (i, 0)),
        out_shape=jax.ShapeDtypeStruct((n, dm), F32),
        compiler_params=_params(("parallel", "arbitrary"), 48),
        name="mixout",
    )(x, pool_o, conv_o, attn_o, z, z, z, b_gate, proj_pool, proj_conv, proj_attn, w_out)


def _pool_prompt_kernel(cur_ref, halo_ref, w_ref, scale_ref, o_ref, ext_ref):
    tt = cur_ref.shape[0]
    i = pl.program_id(0)
    ext_ref[0:POOL_HALO, :] = jnp.where(i > 0, halo_ref[...], 0.0)
    ext_ref[POOL_HALO:, :] = cur_ref[...]
    pos = i * tt + lax.broadcasted_iota(jnp.int32, (tt, 1), 0)
    cg = cur_ref.shape[1] // len(POOL_WINDOWS)
    for g, w in enumerate(POOL_WINDOWS):
        cols = slice(g * cg, (g + 1) * cg)
        u = cur_ref[:, cols]
        win = u
        for j in range(1, w):
            win = win + ext_ref[pl.ds(POOL_HALO - j, tt), cols]
        cnt = jnp.minimum(pos + 1, w).astype(F32)
        d = (win / cnt - u).astype(BF16)
        y = jnp.dot(d, w_ref[g], preferred_element_type=F32)
        o_ref[:, cols] = (y * scale_ref[:, cols]).astype(o_ref.dtype)


def _pool_prompt(z, seq, pool_w, pool_scale, layer):
    c_pool = pool_scale.shape[-1]
    tt = SEQ_TILE
    hb = tt // POOL_HALO
    ng, cg = pool_w.shape[1], pool_w.shape[2]
    return pl.pallas_call(
        _pool_prompt_kernel,
        grid=(seq // tt,),
        in_specs=[
            pl.BlockSpec((tt, c_pool), lambda i: (i, 0)),
            pl.BlockSpec((POOL_HALO, c_pool), lambda i: (jnp.maximum(i * hb - 1, 0), 0)),
            pl.BlockSpec((None, ng, cg, cg), lambda i: (layer, 0, 0, 0)),
            pl.BlockSpec((None, 1, c_pool), lambda i: (layer, 0, 0)),
        ],
        out_specs=pl.BlockSpec((tt, c_pool), lambda i: (i, 0)),
        out_shape=jax.ShapeDtypeStruct((seq, c_pool), BF16),
        scratch_shapes=[pltpu.VMEM((tt + POOL_HALO, c_pool), F32)],
        compiler_params=_params(("parallel",), 32),
        name="pool_prompt",
    )(z, z, pool_w, pool_scale)


def _pool_sample_kernel(ext_ref, w_ref, scale_ref, o_ref):
    steps = o_ref.shape[0]
    cg = ext_ref.shape[2] // len(POOL_WINDOWS)
    for g, w in enumerate(POOL_WINDOWS):
        cols = slice(g * cg, (g + 1) * cg)
        for t in range(steps):
            u = ext_ref[POOL_STATE + t, :, cols]
            win = u
            for j in range(1, w):
                win = win + ext_ref[POOL_STATE + t - j, :, cols]
            cnt = float(min(PAST_LEN + t + 1, w))
            d = (win / cnt - u).astype(BF16)
            y = jnp.dot(d, w_ref[g], preferred_element_type=F32)
            o_ref[t, :, cols] = (y * scale_ref[:, cols]).astype(o_ref.dtype)


def _pool_sample(ext, steps, pool_w, pool_scale, layer):
    rows, batch, c_pool = ext.shape
    bt = SAMPLE_BATCH_TILE
    ng, cg = pool_w.shape[1], pool_w.shape[2]
    return pl.pallas_call(
        _pool_sample_kernel,
        grid=(batch // bt,),
        in_specs=[
            pl.BlockSpec((rows, bt, c_pool), lambda i: (0, i, 0)),
            pl.BlockSpec((None, ng, cg, cg), lambda i: (layer, 0, 0, 0)),
            pl.BlockSpec((None, 1, c_pool), lambda i: (layer, 0, 0)),
        ],
        out_specs=pl.BlockSpec((steps, bt, c_pool), lambda i: (0, i, 0)),
        out_shape=jax.ShapeDtypeStruct((steps, batch, c_pool), BF16),
        compiler_params=_params(("parallel",), 32),
        name="pool_sample",
    )(ext, pool_w, pool_scale)


def _layernorm_swish(cf, ln_g, ln_b):
    mu = jnp.mean(cf, axis=-1, keepdims=True)
    dev = cf - mu
    var = jnp.mean(dev * dev, axis=-1, keepdims=True)
    cn = dev * lax.rsqrt(var + EPS) * ln_g + ln_b
    return cn * _sigmoid(cn)


def _conv_prompt_kernel(a_ref, b_ref, ha_ref, hb_ref, w_ref, cb_ref, lg_ref, lb_ref,
                        o_ref, tail_ref, ext_ref, cf_ref):
    tt, ch = a_ref.shape
    i = pl.program_id(0)
    halo = ha_ref[...] * _sigmoid(hb_ref[...])
    ext_ref[0:CONV_HALO, :] = jnp.where(i > 0, halo, 0.0)
    ext_ref[CONV_HALO:, :] = a_ref[...] * _sigmoid(b_ref[...])
    first = CONV_HALO - CONV_STATE
    lanes = 128

    def col_block(cb, carry):
        cols = pl.ds(pl.multiple_of(cb * lanes, lanes), lanes)
        for r0 in range(0, tt, CONV_ROWS):
            acc = jnp.zeros((CONV_ROWS, lanes), F32)
            for j in range(CONV_WIDTH):
                acc = acc + ext_ref[pl.ds(r0 + first + j, CONV_ROWS), cols] * w_ref[pl.ds(j, 1), cols]
            cf_ref[pl.ds(r0, CONV_ROWS), cols] = acc + cb_ref[:, cols]
        return carry

    lax.fori_loop(0, ch // lanes, col_block, 0)
    o_ref[...] = _layernorm_swish(cf_ref[...], lg_ref[...], lb_ref[...]).astype(o_ref.dtype)
    tail_ref[...] = ext_ref[pl.ds(tt, CONV_HALO), :]


def _conv_prompt(z, seq, a_col_block, conv_w, conv_b, ln_g, ln_b, layer):
    ch = conv_w.shape[-1]
    tt = SEQ_TILE
    hb = tt // CONV_HALO

    def vec_spec():
        return pl.BlockSpec((None, 1, ch), lambda i: (layer, 0, 0))

    return pl.pallas_call(
        _conv_prompt_kernel,
        grid=(seq // tt,),
        in_specs=[
            pl.BlockSpec((tt, ch), lambda i: (i, a_col_block)),
            pl.BlockSpec((tt, ch), lambda i: (i, a_col_block + 1)),
            pl.BlockSpec((CONV_HALO, ch), lambda i: (jnp.maximum(i * hb - 1, 0), a_col_block)),
            pl.BlockSpec((CONV_HALO, ch), lambda i: (jnp.maximum(i * hb - 1, 0), a_col_block + 1)),
            pl.BlockSpec((None, CONV_WIDTH, ch), lambda i: (layer, 0, 0)),
            vec_spec(), vec_spec(), vec_spec(),
        ],
        out_specs=[
            pl.BlockSpec((tt, ch), lambda i: (i, 0)),
            pl.BlockSpec((CONV_HALO, ch), lambda i: (0, 0)),
        ],
        out_shape=[
            jax.ShapeDtypeStruct((seq, ch), BF16),
            jax.ShapeDtypeStruct((CONV_HALO, ch), F32),
        ],
        scratch_shapes=[pltpu.VMEM((tt + CONV_HALO, ch), F32), pltpu.VMEM((tt, ch), F32)],
        compiler_params=_params(("arbitrary",), 40),
        name="conv_prompt",
    )(z, z, z, z, conv_w, conv_b, ln_g, ln_b)


def _conv_sample_kernel(a_ref, b_ref, st_ref, w_ref, cb_ref, lg_ref, lb_ref, o_ref, glu_ref, cf_ref):
    steps, bt, ch = a_ref.shape
    glu_ref[...] = a_ref[...] * _sigmoid(b_ref[...])
    lanes = 256

    def ext_row(r, cols):
        if r < CONV_STATE:
            return st_ref[r, :, cols]
        return glu_ref[r - CONV_STATE, :, cols]

    for t in range(steps):
        for c0 in range(0, ch, lanes):
            cols = slice(c0, c0 + lanes)
            acc = jnp.zeros((bt, lanes), F32)
            for j in range(CONV_WIDTH):
                acc = acc + ext_row(t + j, cols) * w_ref[pl.ds(j, 1), cols]
            cf_ref[:, cols] = acc + cb_ref[:, cols]
        o_ref[t] = _layernorm_swish(cf_ref[...], lg_ref[...], lb_ref[...]).astype(o_ref.dtype)


def _conv_sample(a_t, b_t, state_t, conv_w, conv_b, ln_g, ln_b, layer):
    steps, batch, ch = a_t.shape
    bt = SAMPLE_BATCH_TILE

    def vec_spec():
        return pl.BlockSpec((None, 1, ch), lambda i: (layer, 0, 0))

    return pl.pallas_call(
        _conv_sample_kernel,
        grid=(batch // bt,),
        in_specs=[
            pl.BlockSpec((steps, bt, ch), lambda i: (0, i, 0)),
            pl.BlockSpec((steps, bt, ch), lambda i: (0, i, 0)),
            pl.BlockSpec((CONV_STATE, bt, ch), lambda i: (0, i, 0)),
            pl.BlockSpec((None, CONV_WIDTH, ch), lambda i: (layer, 0, 0)),
            vec_spec(), vec_spec(), vec_spec(),
        ],
        out_specs=[
            pl.BlockSpec((steps, bt, ch), lambda i: (0, i, 0)),
            pl.BlockSpec((steps, bt, ch), lambda i: (0, i, 0)),
        ],
        out_shape=[
            jax.ShapeDtypeStruct((steps, batch, ch), BF16),
            jax.ShapeDtypeStruct((steps, batch, ch), F32),
        ],
        scratch_shapes=[pltpu.VMEM((bt, ch), F32)],
        compiler_params=_params(("parallel",), 32),
        name="conv_sample",
    )(a_t, b_t, state_t, conv_w, conv_b, ln_g, ln_b)


def _attn_prompt_kernel(*refs):
    q_refs = refs[0:3]
    kc_refs = refs[3:6]
    kh_refs = refs[6:9]
    vc_refs = refs[9:12]
    vh_refs = refs[12:15]
    o_refs = refs[15:18]
    lse_refs = refs[18:21]
    chunk = pl.program_id(1)
    seq_rows = pl.num_programs(1) * ATT_CHUNK
    scale = HEAD_DIM ** -0.5
    qi = lax.broadcasted_iota(jnp.int32, (ATT_BLOCK, 2 * ATT_BLOCK), 0)
    kj = lax.broadcasted_iota(jnp.int32, (ATT_BLOCK, 2 * ATT_BLOCK), 1)
    band = (kj >= qi) & (kj <= qi + KEYS_PER_QUERY)
    for g, (_, d) in enumerate(ATTN_GROUPS):
        chunks_per_class = seq_rows // d // ATT_CHUNK
        class_start = (chunk % chunks_per_class) == 0
        kext = jnp.concatenate([kh_refs[g][...], kc_refs[g][...]], axis=0).astype(BF16)
        vext = jnp.concatenate([vh_refs[g][...], vc_refs[g][...]], axis=0).astype(BF16)
        for sb in range(ATT_CHUNK // ATT_BLOCK):
            rows = slice(sb * ATT_BLOCK, (sb + 1) * ATT_BLOCK)
            q = q_refs[g][rows, :].astype(BF16)
            k = kext[sb * ATT_BLOCK:(sb + 2) * ATT_BLOCK]
            v = vext[sb * ATT_BLOCK:(sb + 2) * ATT_BLOCK]
            s = lax.dot_general(q, k, (((1,), (1,)), ((), ())), preferred_element_type=F32) * scale
            valid = band
            if sb == 0:
                valid = band & (kj >= jnp.where(class_start, ATT_BLOCK, 0))
            s = jnp.where(valid, s, MASKED)
            m = jnp.max(s, axis=-1, keepdims=True)
            e = jnp.exp(s - m)
            den = jnp.sum(e, axis=-1, keepdims=True)
            o = jnp.dot(e.astype(BF16), v, preferred_element_type=F32) / den
            o_refs[g][rows, :] = o
            lse_refs[g][rows, :] = jnp.broadcast_to(m + jnp.log(den), (ATT_BLOCK, HEAD_DIM))


def _attn_prompt(q_cls, k_cls, v_cls):
    heads, seq, _ = q_cls[0].shape
    per = ATT_CHUNK // ATT_BLOCK
    cur = pl.BlockSpec((None, ATT_CHUNK, HEAD_DIM), lambda h, c: (h, c, 0))
    halo = pl.BlockSpec((None, ATT_BLOCK, HEAD_DIM), lambda h, c: (h, jnp.maximum(c * per - 1, 0), 0))
    out_sds = jax.ShapeDtypeStruct((heads, seq, HEAD_DIM), F32)
    outs = pl.pallas_call(
        _attn_prompt_kernel,
        grid=(heads, seq // ATT_CHUNK),
        in_specs=[cur] * 3 + [cur] * 3 + [halo] * 3 + [cur] * 3 + [halo] * 3,
        out_specs=[cur] * 6,
        out_shape=[out_sds] * 6,
        compiler_params=_params(("parallel", "parallel"), 32),
        name="attn_prompt",
    )(*q_cls, *k_cls, *k_cls, *v_cls, *v_cls)
    return outs[0:3], outs[3:6]


def _merge_kernel(o0, o1, o2, l0, l1, l2, out_ref):
    m = jnp.maximum(jnp.maximum(l0[...], l1[...]), l2[...])
    w0 = jnp.exp(l0[...] - m)
    w1 = jnp.exp(l1[...] - m)
    w2 = jnp.exp(l2[...] - m)
    num = w0 * o0[...] + w1 * o1[...] + w2 * o2[...]
    out_ref[...] = (num / (w0 + w1 + w2)).astype(out_ref.dtype)


def _merge_groups(outs, lses):
    seq, width = outs[0].shape
    spec = pl.BlockSpec((SEQ_TILE, width), lambda i: (i, 0))
    return pl.pallas_call(
        _merge_kernel,
        grid=(seq // SEQ_TILE,),
        in_specs=[spec] * 6,
        out_specs=spec,
        out_shape=jax.ShapeDtypeStruct((seq, width), BF16),
        compiler_params=_params(("parallel",), 32),
        name="attn_merge",
    )(*outs, *lses)


def _to_classes(t, d):
    seq = t.shape[0]
    t = t.reshape(seq // d, d, HEADS_PER_GROUP, HEAD_DIM)
    return jnp.transpose(t, (2, 1, 0, 3)).reshape(HEADS_PER_GROUP, seq, HEAD_DIM)


def _from_classes(t, d):
    heads, seq, _ = t.shape
    t = t.reshape(heads, d, seq // d, HEAD_DIM)
    return jnp.transpose(t, (2, 1, 0, 3)).reshape(seq, heads * HEAD_DIM)


def _attn_sample_kernel(qa_ref, qd_ref, kn_ref, vn_ref, k1_ref, v1_ref, k4_ref, v4_ref, k16_ref, v16_ref, o_ref):
    bt = o_ref.shape[0]
    steps = qd_ref.shape[1]
    slab = steps * HEADS_PER_GROUP
    scale = HEAD_DIM ** -0.5
    cache_refs = ((k1_ref, v1_ref), (k4_ref, v4_ref), (k16_ref, v16_ref))
    srow = lax.broadcasted_iota(jnp.int32, (slab, 1), 0)

    def lanesum(x):
        return jnp.sum(x, axis=-1, keepdims=True)

    def fold(x, op):
        wide = jnp.broadcast_to(x, (slab, HEAD_DIM))
        half = op(wide[0:slab // 2], wide[slab // 2:slab])
        half = op(half, pltpu.roll(half, HEADS_PER_GROUP, 0))
        full = jnp.concatenate([half, half], axis=0)
        return full[:, 0:x.shape[-1]]

    def per_sequence(b, carry):
        m_g, l_g, acc_g = [], [], []
        for g, (w, d) in enumerate(ATTN_GROUPS):
            k_ref, v_ref = cache_refs[g]
            kn = kn_ref[b, g]
            vn = vn_ref[b, g]
            if d % steps == 0:
                qq = qa_ref[b, g]
                s = lanesum(k_ref[b] * qq[None]) * scale
                s_new = lanesum(kn * qq) * scale
                m = jnp.maximum(jnp.max(s, axis=0), s_new)
                e = jnp.exp(s - m[None])
                e_new = jnp.exp(s_new - m)
                l_g.append(jnp.sum(e, axis=0) + e_new)
                acc_g.append(jnp.sum(e * v_ref[b], axis=0) + e_new * vn)
                m_g.append(m)
            else:
                assert d == 1 and w == k_ref.shape[1] * steps
                slabs = k_ref.shape[1]
                pos = (steps * lax.broadcasted_iota(jnp.int32, (slabs, slab, 1), 0)
                       + lax.broadcasted_iota(jnp.int32, (slabs, slab, 1), 1) // HEADS_PER_GROUP)
                m_all = jnp.zeros((slab, 1), F32)
                l_all = jnp.zeros((slab, 1), F32)
                acc_all = jnp.zeros((slab, HEAD_DIM), F32)
                for i in range(steps):
                    qq = qd_ref[b, i]
                    s = jnp.where(pos >= i, lanesum(k_ref[b] * qq[None]) * scale, MASKED)
                    s_new = jnp.where(srow // HEADS_PER_GROUP <= i, lanesum(kn * qq) * scale, MASKED)
                    m = fold(jnp.maximum(jnp.max(s, axis=0), s_new), jnp.maximum)
                    e = jnp.exp(s - m[None])
                    e_new = jnp.exp(s_new - m)
                    l = fold(jnp.sum(e, axis=0) + e_new, jnp.add)
                    acc = fold(jnp.sum(e * v_ref[b], axis=0) + e_new * vn, jnp.add)
                    mine = (srow // HEADS_PER_GROUP) == i
                    m_all = jnp.where(mine, m, m_all)
                    l_all = jnp.where(mine, l, l_all)
                    acc_all = jnp.where(mine, acc, acc_all)
                m_g.append(m_all)
                l_g.append(l_all)
                acc_g.append(acc_all)
        m = jnp.maximum(jnp.maximum(m_g[0], m_g[1]), m_g[2])
        num = jnp.zeros((slab, HEAD_DIM), F32)
        den = jnp.zeros((slab, 1), F32)
        for g in range(N_GROUPS):
            wg = jnp.exp(m_g[g] - m)
            num = num + wg * acc_g[g]
            den = den + wg * l_g[g]
        o_ref[b] = (num / den).astype(o_ref.dtype)
        return carry

    lax.fori_loop(0, bt, per_sequence, 0)


def _attn_sample(q_aligned, q_dense, k_new, v_new, caches_k, caches_v, layer):
    batch = q_aligned.shape[0]
    steps = q_dense.shape[1]
    slab = steps * HEADS_PER_GROUP
    bt = SAMPLE_ATTN_TILE
    small = lambda n: pl.BlockSpec((bt, n, slab, HEAD_DIM), lambda i: (i, 0, 0, 0))
    in_specs = [small(N_GROUPS), small(steps), small(N_GROUPS), small(N_GROUPS)]
    args = [q_aligned, q_dense, k_new, v_new]
    for g, (w, d) in enumerate(ATTN_GROUPS):
        for cache in (caches_k[g], caches_v[g]):
            depth, cb, length, heads, hd = cache.shape
            assert cb == batch and length == w and heads == HEADS_PER_GROUP and hd == HEAD_DIM
            per_slab = max(d, steps)
            view = cache.reshape(depth, batch, length // per_slab, per_slab * heads, hd)
            args.append(view)
            in_specs.append(pl.BlockSpec((None, bt, length // per_slab, slab, hd),
                                         lambda i: (layer, i, 0, 0, 0)))
    return pl.pallas_call(
        _attn_sample_kernel,
        grid=(batch // bt,),
        in_specs=in_specs,
        out_specs=pl.BlockSpec((bt, slab, HEAD_DIM), lambda i: (i, 0, 0)),
        out_shape=jax.ShapeDtypeStruct((batch, slab, HEAD_DIM), BF16),
        compiler_params=_params(("parallel",), 48),
        name="attn_sample",
    )(*args)


def _rope_tables(seq, batch, steps):
    half = ROT_DIM // 2
    inv = ROPE_THETA ** (-jnp.arange(half, dtype=F32) / half)
    pos = jnp.concatenate([jnp.arange(seq), jnp.tile(PAST_LEN + jnp.arange(steps), batch)])
    ang = pos.astype(F32)[:, None] * inv[None, :]
    cos, sin = jnp.cos(ang), jnp.sin(ang)
    rest = HEAD_DIM - ROT_DIM
    n = pos.shape[0]
    cos_t = jnp.concatenate([cos, cos, jnp.ones((n, rest), F32)], axis=1)
    sin_t = jnp.concatenate([-sin, sin, jnp.zeros((n, rest), F32)], axis=1)
    return cos_t, sin_t


def kernel(x_prompt, x_sample, cache_k_d1, cache_v_d1, cache_k_d4, cache_v_d4, cache_k_d16, cache_v_d16,
           state_pool, state_conv, ffn1_norm, ffn1_wg, ffn1_wu, ffn1_wd, mix_norm, w_in, b_gate, pool_w,
           pool_scale, conv_w, conv_b, conv_ln_g, conv_ln_b, q_norm, k_norm, proj_pool, proj_conv, proj_attn,
           w_out, ffn2_norm, ffn2_wg, ffn2_wu, ffn2_wd):
    pb, seq, dm = x_prompt.shape
    batch, steps, _ = x_sample.shape
    depth = w_in.shape[0]
    c_pool = pool_scale.shape[-1]
    c_conv = conv_w.shape[-1]
    c_attn = N_HEADS * HEAD_DIM
    grp = HEADS_PER_GROUP * HEAD_DIM
    assert pb == 1 and steps * HEADS_PER_GROUP == 16
    assert all(w == KEYS_PER_QUERY * d for w, d in ATTN_GROUPS)
    assert c_pool == c_conv
    n_prompt = pb * seq
    n_sample = batch * steps
    col_a = c_pool
    col_q = c_pool + 2 * c_conv
    col_v = col_q + 2 * c_attn
    col_gate = col_v + c_attn
    assert col_q % (2 * c_attn) == 0 and col_gate % MIX_TILE == 0

    caches_k = (cache_k_d1, cache_k_d4, cache_k_d16)
    caches_v = (cache_v_d1, cache_v_d4, cache_v_d16)
    bf = lambda t: t.astype(BF16)
    row = lambda t: t.reshape(depth, 1, t.shape[-1])
    ffn1 = (row(ffn1_norm), bf(ffn1_wg), bf(ffn1_wu), bf(ffn1_wd))
    ffn2 = (row(ffn2_norm), bf(ffn2_wg), bf(ffn2_wu), bf(ffn2_wd))
    w_in_b, pool_w_b = bf(w_in), bf(pool_w)
    pp_b, pc_b, pa_b, wo_b = bf(proj_pool), bf(proj_conv), bf(proj_attn), bf(w_out)
    mix_gain, pool_scale_r = row(mix_norm), row(pool_scale)
    conv_b_r, ln_g_r, ln_b_r = row(conv_b), row(conv_ln_g), row(conv_ln_b)
    b_gate_r = b_gate.reshape(depth, 3, dm)
    cos_t, sin_t = _rope_tables(seq, batch, steps)

    x = jnp.concatenate([x_prompt.reshape(n_prompt, dm), x_sample.reshape(n_sample, dm)], axis=0)
    tm_major = lambda t: jnp.transpose(t, (1, 0, 2))

    outs = {name: [] for name in ("pk", "pv", "ppool", "pconv", "sk", "sv", "spool", "sconv")}
    for l in range(depth):
        x = _ffn(x, *ffn1, l)
        z = _inproj(x, mix_gain, w_in_b, l)
        qk = _qknorm(z, jnp.stack([q_norm[l], k_norm[l]]), cos_t, sin_t, col_q // (2 * c_attn))
        v_all = z[:, col_v:col_v + c_attn]

        pool_p = _pool_prompt(z, seq, pool_w_b, pool_scale_r, l)
        zp_s = z[n_prompt:, :c_pool].reshape(batch, steps, c_pool)
        pool_ext = jnp.concatenate([state_pool[l], zp_s], axis=1)
        pool_s = _pool_sample(tm_major(pool_ext), steps, pool_w_b, pool_scale_r, l)
        pool_o = jnp.concatenate([pool_p, tm_major(pool_s).reshape(n_sample, c_pool)], axis=0)
        outs["ppool"].append(z[n_prompt - POOL_STATE:n_prompt, :c_pool].reshape(pb, POOL_STATE, c_pool))
        outs["spool"].append(pool_ext[:, -POOL_STATE:])

        conv_p, glu_tail = _conv_prompt(z, seq, col_a // c_conv, conv_w, conv_b_r, ln_g_r, ln_b_r, l)
        a_s = z[n_prompt:, col_a:col_a + c_conv].reshape(batch, steps, c_conv)
        b_s = z[n_prompt:, col_a + c_conv:col_a + 2 * c_conv].reshape(batch, steps, c_conv)
        conv_s, glu_s = _conv_sample(tm_major(a_s), tm_major(b_s), tm_major(state_conv[l]),
                                     conv_w, conv_b_r, ln_g_r, ln_b_r, l)
        conv_o = jnp.concatenate([conv_p, tm_major(conv_s).reshape(n_sample, c_conv)], axis=0)
        outs["pconv"].append(glu_tail[CONV_HALO - CONV_STATE:].reshape(pb, CONV_STATE, c_conv))
        outs["sconv"].append(jnp.concatenate([state_conv[l], tm_major(glu_s)], axis=1)[:, -CONV_STATE:])

        q_cls, k_cls, v_cls = [], [], []
        for g, (w, d) in enumerate(ATTN_GROUPS):
            q_cls.append(_to_classes(qk[:n_prompt, g * grp:(g + 1) * grp], d))
            k_cls.append(_to_classes(qk[:n_prompt, c_attn + g * grp:c_attn + (g + 1) * grp], d))
            v_cls.append(_to_classes(v_all[:n_prompt, g * grp:(g + 1) * grp], d))
        o_cls, lse_cls = _attn_prompt(q_cls, k_cls, v_cls)
        attn_p = _merge_groups([_from_classes(o_cls[g], d) for g, (_, d) in enumerate(ATTN_GROUPS)],
                               [_from_classes(lse_cls[g], d) for g, (_, d) in enumerate(ATTN_GROUPS)])

        slab = steps * HEADS_PER_GROUP
        by_group = lambda t: jnp.transpose(t.reshape(batch, steps, N_GROUPS, HEADS_PER_GROUP, HEAD_DIM),
                                           (0, 2, 1, 3, 4))
        q_s = by_group(qk[n_prompt:, :c_attn])
        k_s = by_group(qk[n_prompt:, c_attn:])
        v_s = by_group(v_all[n_prompt:])
        dense = [g for g, (_, d) in enumerate(ATTN_GROUPS) if d % steps != 0]
        assert dense == [0]
        q_dense = jnp.tile(q_s[:, 0], (1, 1, steps, 1))
        attn_s = _attn_sample(q_s.reshape(batch, N_GROUPS, slab, HEAD_DIM), q_dense,
                              k_s.reshape(batch, N_GROUPS, slab, HEAD_DIM),
                              v_s.reshape(batch, N_GROUPS, slab, HEAD_DIM), caches_k, caches_v, l)
        attn_o = jnp.concatenate([attn_p, attn_s.reshape(n_sample, grp)], axis=0)
        for g, (w, d) in enumerate(ATTN_GROUPS):
            keep = min(w, seq)
            kg = qk[n_prompt - keep:n_prompt, c_attn + g * grp:c_attn + (g + 1) * grp]
            vg = v_all[n_prompt - keep:n_prompt, g * grp:(g + 1) * grp]
            outs["pk"].append(kg.reshape(pb, keep, HEADS_PER_GROUP, HEAD_DIM))
            outs["pv"].append(vg.reshape(pb, keep, HEADS_PER_GROUP, HEAD_DIM))
            outs["sk"].append(k_s[:, g])
            outs["sv"].append(v_s[:, g])

        x = _mixout(x, pool_o, conv_o, attn_o, z, col_gate, b_gate_r, pp_b, pc_b, pa_b, wo_b, l)
        x = _ffn(x, *ffn2, l)

    def per_group(name, g):
        return jnp.stack([outs[name][l * N_GROUPS + g] for l in range(depth)], axis=0)

    st = lambda name: jnp.stack(outs[name], axis=0)
    return (x[:n_prompt].reshape(pb, seq, dm), x[n_prompt:].reshape(batch, steps, dm),
            per_group("pk", 0), per_group("pv", 0), per_group("pk", 1), per_group("pv", 1),
            per_group("pk", 2), per_group("pv", 2), st("ppool"), st("pconv"),
            per_group("sk", 0), per_group("sv", 0), per_group("sk", 1), per_group("sv", 1),
            per_group("sk", 2), per_group("sv", 2), st("spool"), st("sconv"))
```

```python
import functools

import numpy as np
import jax
import jax.numpy as jnp
from jax import lax
from jax.experimental import pallas as pl
from jax.experimental.pallas import tpu as pltpu

F32 = jnp.float32
BF16 = jnp.bfloat16

PAST_LEN = 2048
POOL_WINDOWS = (2, 4, 8, 16)
POOL_STATE = max(POOL_WINDOWS) - 1
CONV_WIDTH = 31
CONV_STATE = CONV_WIDTH - 1
HEAD_DIM = 128
ATTN_GROUPS = ((128, 1), (512, 4), (2048, 16))
N_GROUPS = len(ATTN_GROUPS)
HEADS_PER_GROUP = 4
N_HEADS = N_GROUPS * HEADS_PER_GROUP
KEYS_PER_QUERY = 128
ROT_DIM = HEAD_DIM // 4
ROPE_THETA = 500000.0
EPS = 1e-6
MASKED = -1e30

TOKEN_TILE = 512
FF_TILE = 512
IN_TILE = 1536
MIX_TILE = 512
SEQ_TILE = 512
POOL_HALO = 16
CONV_HALO = 32
ATT_TILE = 2048
ATT_BLOCK = 128
SAMPLE_BATCH_TILE = 32
SAMPLE_ATTN_TILE = 4
CONV_ROWS = 64
MIB = 1 << 20


def _params(semantics, vmem_mib):
    return pltpu.CompilerParams(dimension_semantics=semantics, vmem_limit_bytes=vmem_mib * MIB)


def _rms(x, gain):
    return x * lax.rsqrt(jnp.mean(x * x, axis=-1, keepdims=True) + EPS) * gain


def _sigmoid(x):
    return 1.0 / (1.0 + jnp.exp(-x))


def _ffn_kernel(x_ref, gain_ref, wg_ref, wu_ref, wd_ref, o_ref, h_ref):
    @pl.when(pl.program_id(1) == 0)
    def _():
        x = x_ref[...]
        h_ref[...] = _rms(x, gain_ref[...]).astype(BF16)
        o_ref[...] = x

    h = h_ref[...]
    g = jnp.dot(h, wg_ref[...], preferred_element_type=F32)
    u = jnp.dot(h, wu_ref[...], preferred_element_type=F32)
    a = (g * _sigmoid(g) * u).astype(BF16)
    o_ref[...] += 0.5 * jnp.dot(a, wd_ref[...], preferred_element_type=F32)


def _ffn(x, gain, wg, wu, wd, layer):
    n, dm = x.shape
    dff = wg.shape[-1]
    tm, tf = TOKEN_TILE, FF_TILE
    return pl.pallas_call(
        _ffn_kernel,
        grid=(n // tm, dff // tf),
        in_specs=[
            pl.BlockSpec((tm, dm), lambda i, j: (i, 0)),
            pl.BlockSpec((None, 1, dm), lambda i, j: (layer, 0, 0)),
            pl.BlockSpec((None, dm, tf), lambda i, j: (layer, 0, j)),
            pl.BlockSpec((None, dm, tf), lambda i, j: (layer, 0, j)),
            pl.BlockSpec((None, tf, dm), lambda i, j: (layer, j, 0)),
        ],
        out_specs=pl.BlockSpec((tm, dm), lambda i, j: (i, 0)),
        out_shape=jax.ShapeDtypeStruct((n, dm), F32),
        scratch_shapes=[pltpu.VMEM((tm, dm), BF16)],
        compiler_params=_params(("parallel", "arbitrary"), 48),
        name="ffn",
    )(x, gain, wg, wu, wd)


def _inproj_kernel(x_ref, gain_ref, w_ref, z_ref, h_ref):
    @pl.when(pl.program_id(1) == 0)
    def _():
        h_ref[...] = _rms(x_ref[...], gain_ref[...]).astype(BF16)

    z_ref[...] = jnp.dot(h_ref[...], w_ref[...], preferred_element_type=F32)


def _inproj(x, gain, w, layer):
    n, dm = x.shape
    cols = w.shape[-1]
    tm, tn = TOKEN_TILE, IN_TILE
    return pl.pallas_call(
        _inproj_kernel,
        grid=(n // tm, cols // tn),
        in_specs=[
            pl.BlockSpec((tm, dm), lambda i, j: (i, 0)),
            pl.BlockSpec((None, 1, dm), lambda i, j: (layer, 0, 0)),
            pl.BlockSpec((None, dm, tn), lambda i, j: (layer, 0, j)),
        ],
        out_specs=pl.BlockSpec((tm, tn), lambda i, j: (i, j)),
        out_shape=jax.ShapeDtypeStruct((n, cols), F32),
        scratch_shapes=[pltpu.VMEM((tm, dm), BF16)],
        compiler_params=_params(("parallel", "arbitrary"), 48),
        name="inproj",
    )(x, gain, w)


def _qknorm_kernel(z_ref, gain_ref, cos_ref, sin_ref, o_ref):
    half = ROT_DIM // 2
    c = cos_ref[...]
    s = sin_ref[...]
    lane = lax.broadcasted_iota(jnp.int32, c.shape, 1)
    for hh in range(2 * N_HEADS):
        cols = slice(hh * HEAD_DIM, (hh + 1) * HEAD_DIM)
        y = _rms(z_ref[:, cols], gain_ref[pl.ds(hh // N_HEADS, 1), :])
        partner = jnp.where(lane < half, pltpu.roll(y, HEAD_DIM - half, 1), pltpu.roll(y, half, 1))
        o_ref[:, cols] = y * c + partner * s


def _qknorm(z, gains, cos_t, sin_t, qk_col_block):
    n = z.shape[0]
    width = 2 * N_HEADS * HEAD_DIM
    tm = TOKEN_TILE
    return pl.pallas_call(
        _qknorm_kernel,
        grid=(n // tm,),
        in_specs=[
            pl.BlockSpec((tm, width), lambda i: (i, qk_col_block)),
            pl.BlockSpec((2, HEAD_DIM), lambda i: (0, 0)),
            pl.BlockSpec((tm, HEAD_DIM), lambda i: (i, 0)),
            pl.BlockSpec((tm, HEAD_DIM), lambda i: (i, 0)),
        ],
        out_specs=pl.BlockSpec((tm, width), lambda i: (i, 0)),
        out_shape=jax.ShapeDtypeStruct((n, width), F32),
        compiler_params=_params(("parallel",), 40),
        name="qknorm_rope",
    )(z, gains, cos_t, sin_t)


def _mixout_kernel(x_ref, pool_p_ref, pool_s_ref, conv_p_ref, conv_s_ref, attn_p_ref, attn_s_ref,
                   zg0_ref, zg1_ref, zg2_ref, bg_ref, pp_ref, pc_ref, pa_ref, wo_ref, o_ref, *, prompt_tiles):
    @pl.when(pl.program_id(1) == 0)
    def _():
        o_ref[...] = x_ref[...]

    is_prompt = pl.program_id(0) < prompt_tiles

    def gated(zg_ref, b, p_ref, s_ref, w_ref):
        branch = jnp.where(is_prompt, p_ref[...], s_ref[...])
        gate = _sigmoid(zg_ref[...] + bg_ref[pl.ds(b, 1), :])
        return gate * jnp.dot(branch, w_ref[...], preferred_element_type=F32)

    merged = (gated(zg0_ref, 0, pool_p_ref, pool_s_ref, pp_ref)
              + gated(zg1_ref, 1, conv_p_ref, conv_s_ref, pc_ref)
              + gated(zg2_ref, 2, attn_p_ref, attn_s_ref, pa_ref))
    o_ref[...] += jnp.dot(merged.astype(BF16), wo_ref[...], preferred_element_type=F32)


def _mixout(x, pool_ps, conv_ps, attn_ps, z, gate_col0, b_gate, proj_pool, proj_conv, proj_attn, w_out, layer):
    n, dm = x.shape
    tm, tn = TOKEN_TILE, MIX_TILE
    g0 = gate_col0 // tn
    per = dm // tn
    prompt_tiles = pool_ps[0].shape[0] // tm
    assert all(p.shape[0] == prompt_tiles * tm and s.shape[0] == tm for p, s in (pool_ps, conv_ps, attn_ps))
    assert n == (prompt_tiles + 1) * tm

    def zg_spec(b):
        return pl.BlockSpec((tm, tn), lambda i, j: (i, g0 + b * per + j))

    def branch_specs(pair):
        width = pair[0].shape[1]
        return [pl.BlockSpec((tm, width), lambda i, j: (jnp.minimum(i, prompt_tiles - 1), 0)),
                pl.BlockSpec((tm, width), lambda i, j: (0, 0))]

    return pl.pallas_call(
        functools.partial(_mixout_kernel, prompt_tiles=prompt_tiles),
        grid=(n // tm, per),
        in_specs=[
            pl.BlockSpec((tm, dm), lambda i, j: (i, 0)),
            *branch_specs(pool_ps), *branch_specs(conv_ps), *branch_specs(attn_ps),
            zg_spec(0), zg_spec(1), zg_spec(2),
            pl.BlockSpec((None, 3, tn), lambda i, j: (layer, 0, j)),
            pl.BlockSpec((None, proj_pool.shape[1], tn), lambda i, j: (layer, 0, j)),
            pl.BlockSpec((None, proj_conv.shape[1], tn), lambda i, j: (layer, 0, j)),
            pl.BlockSpec((None, proj_attn.shape[1], tn), lambda i, j: (layer, 0, j)),
            pl.BlockSpec((None, tn, dm), lambda i, j: (layer, j, 0)),
        ],
        out_specs=pl.BlockSpec((tm, dm), lambda i, j: (i, 0)),
        out_shape=jax.ShapeDtypeStruct((n, dm), F32),
        compiler_params=_params(("parallel", "arbitrary"), 48),
        name="mixout",
    )(x, *pool_ps, *conv_ps, *attn_ps, z, z, z, b_gate, proj_pool, proj_conv, proj_attn, w_out)


def _pool_prompt_kernel(cur_ref, halo_ref, w_ref, scale_ref, o_ref, ext_ref):
    tt = cur_ref.shape[0]
    i = pl.program_id(0)
    ext_ref[0:POOL_HALO, :] = jnp.where(i > 0, halo_ref[...], 0.0)
    ext_ref[POOL_HALO:, :] = cur_ref[...]
    pos = i * tt + lax.broadcasted_iota(jnp.int32, (tt, 1), 0)
    cg = cur_ref.shape[1] // len(POOL_WINDOWS)
    for g, w in enumerate(POOL_WINDOWS):
        cols = slice(g * cg, (g + 1) * cg)
        u = cur_ref[:, cols]
        win = u
        for j in range(1, w):
            win = win + ext_ref[pl.ds(POOL_HALO - j, tt), cols]
        cnt = jnp.minimum(pos + 1, w).astype(F32)
        d = (win / cnt - u).astype(BF16)
        y = jnp.dot(d, w_ref[g], preferred_element_type=F32)
        o_ref[:, cols] = (y * scale_ref[:, cols]).astype(o_ref.dtype)


def _pool_prompt(z, seq, pool_w, pool_scale, layer):
    c_pool = pool_scale.shape[-1]
    tt = SEQ_TILE
    hb = tt // POOL_HALO
    ng, cg = pool_w.shape[1], pool_w.shape[2]
    return pl.pallas_call(
        _pool_prompt_kernel,
        grid=(seq // tt,),
        in_specs=[
            pl.BlockSpec((tt, c_pool), lambda i: (i, 0)),
            pl.BlockSpec((POOL_HALO, c_pool), lambda i: (jnp.maximum(i * hb - 1, 0), 0)),
            pl.BlockSpec((None, ng, cg, cg), lambda i: (layer, 0, 0, 0)),
            pl.BlockSpec((None, 1, c_pool), lambda i: (layer, 0, 0)),
        ],
        out_specs=pl.BlockSpec((tt, c_pool), lambda i: (i, 0)),
        out_shape=jax.ShapeDtypeStruct((seq, c_pool), BF16),
        scratch_shapes=[pltpu.VMEM((tt + POOL_HALO, c_pool), F32)],
        compiler_params=_params(("parallel",), 32),
        name="pool_prompt",
    )(z, z, pool_w, pool_scale)


def _pool_sample_kernel(ext_ref, w_ref, scale_ref, o_ref):
    steps = o_ref.shape[0]
    cg = ext_ref.shape[2] // len(POOL_WINDOWS)
    for g, w in enumerate(POOL_WINDOWS):
        cols = slice(g * cg, (g + 1) * cg)
        for t in range(steps):
            u = ext_ref[POOL_STATE + t, :, cols]
            win = u
            for j in range(1, w):
                win = win + ext_ref[POOL_STATE + t - j, :, cols]
            cnt = float(min(PAST_LEN + t + 1, w))
            d = (win / cnt - u).astype(BF16)
            y = jnp.dot(d, w_ref[g], preferred_element_type=F32)
            o_ref[t, :, cols] = (y * scale_ref[:, cols]).astype(o_ref.dtype)


def _pool_sample(ext, steps, pool_w, pool_scale, layer):
    rows, batch, c_pool = ext.shape
    bt = SAMPLE_BATCH_TILE
    ng, cg = pool_w.shape[1], pool_w.shape[2]
    return pl.pallas_call(
        _pool_sample_kernel,
        grid=(batch // bt,),
        in_specs=[
            pl.BlockSpec((rows, bt, c_pool), lambda i: (0, i, 0)),
            pl.BlockSpec((None, ng, cg, cg), lambda i: (layer, 0, 0, 0)),
            pl.BlockSpec((None, 1, c_pool), lambda i: (layer, 0, 0)),
        ],
        out_specs=pl.BlockSpec((steps, bt, c_pool), lambda i: (0, i, 0)),
        out_shape=jax.ShapeDtypeStruct((steps, batch, c_pool), BF16),
        compiler_params=_params(("parallel",), 32),
        name="pool_sample",
    )(ext, pool_w, pool_scale)


def _layernorm_swish(cf, ln_g, ln_b):
    mu = jnp.mean(cf, axis=-1, keepdims=True)
    dev = cf - mu
    var = jnp.mean(dev * dev, axis=-1, keepdims=True)
    cn = dev * lax.rsqrt(var + EPS) * ln_g + ln_b
    return cn * _sigmoid(cn)


def _conv_prompt_kernel(a_ref, b_ref, ha_ref, hb_ref, w_ref, cb_ref, lg_ref, lb_ref,
                        o_ref, tail_ref, ext_ref, cf_ref):
    tt, ch = a_ref.shape
    i = pl.program_id(0)
    halo = ha_ref[...] * _sigmoid(hb_ref[...])
    ext_ref[0:CONV_HALO, :] = jnp.where(i > 0, halo, 0.0)
    ext_ref[CONV_HALO:, :] = a_ref[...] * _sigmoid(b_ref[...])
    first = CONV_HALO - CONV_STATE
    lanes = 128

    def col_block(cb, carry):
        cols = pl.ds(pl.multiple_of(cb * lanes, lanes), lanes)
        for r0 in range(0, tt, CONV_ROWS):
            acc = jnp.zeros((CONV_ROWS, lanes), F32)
            for j in range(CONV_WIDTH):
                acc = acc + ext_ref[pl.ds(r0 + first + j, CONV_ROWS), cols] * w_ref[pl.ds(j, 1), cols]
            cf_ref[pl.ds(r0, CONV_ROWS), cols] = acc + cb_ref[:, cols]
        return carry

    lax.fori_loop(0, ch // lanes, col_block, 0)
    o_ref[...] = _layernorm_swish(cf_ref[...], lg_ref[...], lb_ref[...]).astype(o_ref.dtype)
    tail_ref[...] = ext_ref[pl.ds(tt, CONV_HALO), :]


def _conv_prompt(z, seq, a_col_block, conv_w, conv_b, ln_g, ln_b, layer):
    ch = conv_w.shape[-1]
    tt = SEQ_TILE
    hb = tt // CONV_HALO

    def vec_spec():
        return pl.BlockSpec((None, 1, ch), lambda i: (layer, 0, 0))

    return pl.pallas_call(
        _conv_prompt_kernel,
        grid=(seq // tt,),
        in_specs=[
            pl.BlockSpec((tt, ch), lambda i: (i, a_col_block)),
            pl.BlockSpec((tt, ch), lambda i: (i, a_col_block + 1)),
            pl.BlockSpec((CONV_HALO, ch), lambda i: (jnp.maximum(i * hb - 1, 0), a_col_block)),
            pl.BlockSpec((CONV_HALO, ch), lambda i: (jnp.maximum(i * hb - 1, 0), a_col_block + 1)),
            pl.BlockSpec((None, CONV_WIDTH, ch), lambda i: (layer, 0, 0)),
            vec_spec(), vec_spec(), vec_spec(),
        ],
        out_specs=[
            pl.BlockSpec((tt, ch), lambda i: (i, 0)),
            pl.BlockSpec((CONV_HALO, ch), lambda i: (0, 0)),
        ],
        out_shape=[
            jax.ShapeDtypeStruct((seq, ch), BF16),
            jax.ShapeDtypeStruct((CONV_HALO, ch), F32),
        ],
        scratch_shapes=[pltpu.VMEM((tt + CONV_HALO, ch), F32), pltpu.VMEM((tt, ch), F32)],
        compiler_params=_params(("arbitrary",), 40),
        name="conv_prompt",
    )(z, z, z, z, conv_w, conv_b, ln_g, ln_b)


def _conv_sample_kernel(a_ref, b_ref, st_ref, w_ref, cb_ref, lg_ref, lb_ref, o_ref, glu_ref, cf_ref):
    steps, bt, ch = a_ref.shape
    glu_ref[...] = a_ref[...] * _sigmoid(b_ref[...])
    lanes = 256

    def ext_row(r, cols):
        if r < CONV_STATE:
            return st_ref[r, :, cols]
        return glu_ref[r - CONV_STATE, :, cols]

    for t in range(steps):
        for c0 in range(0, ch, lanes):
            cols = slice(c0, c0 + lanes)
            acc = jnp.zeros((bt, lanes), F32)
            for j in range(CONV_WIDTH):
                acc = acc + ext_row(t + j, cols) * w_ref[pl.ds(j, 1), cols]
            cf_ref[:, cols] = acc + cb_ref[:, cols]
        o_ref[t] = _layernorm_swish(cf_ref[...], lg_ref[...], lb_ref[...]).astype(o_ref.dtype)


def _conv_sample(a_t, b_t, state_t, conv_w, conv_b, ln_g, ln_b, layer):
    steps, batch, ch = a_t.shape
    bt = SAMPLE_BATCH_TILE

    def vec_spec():
        return pl.BlockSpec((None, 1, ch), lambda i: (layer, 0, 0))

    return pl.pallas_call(
        _conv_sample_kernel,
        grid=(batch // bt,),
        in_specs=[
            pl.BlockSpec((steps, bt, ch), lambda i: (0, i, 0)),
            pl.BlockSpec((steps, bt, ch), lambda i: (0, i, 0)),
            pl.BlockSpec((CONV_STATE, bt, ch), lambda i: (0, i, 0)),
            pl.BlockSpec((None, CONV_WIDTH, ch), lambda i: (layer, 0, 0)),
            vec_spec(), vec_spec(), vec_spec(),
        ],
        out_specs=[
            pl.BlockSpec((steps, bt, ch), lambda i: (0, i, 0)),
            pl.BlockSpec((steps, bt, ch), lambda i: (0, i, 0)),
        ],
        out_shape=[
            jax.ShapeDtypeStruct((steps, batch, ch), BF16),
            jax.ShapeDtypeStruct((steps, batch, ch), F32),
        ],
        scratch_shapes=[pltpu.VMEM((bt, ch), F32)],
        compiler_params=_params(("parallel",), 32),
        name="conv_sample",
    )(a_t, b_t, state_t, conv_w, conv_b, ln_g, ln_b)


def _attn_prompt_kernel(*refs):
    q_refs = refs[0:3]
    kc_refs = refs[3:6]
    kh_refs = refs[6:9]
    vc_refs = refs[9:12]
    vh_refs = refs[12:15]
    o_ref, num_ref, m_ref, l_ref = refs[15:19]
    tile = pl.program_id(1)
    scale = HEAD_DIM ** -0.5
    blk = ATT_BLOCK
    qi = lax.broadcasted_iota(jnp.int32, (blk, blk), 0)
    kj = lax.broadcasted_iota(jnp.int32, (blk, blk), 1)
    cur_ok = kj <= qi
    prev_ok = kj >= qi
    prev_ok_first = kj >= qi + jnp.where(tile > 0, 0, 2 * blk)

    def nt_dot(a, b):
        return lax.dot_general(a, b, (((1,), (1,)), ((), ())), preferred_element_type=F32)

    def wide(x):
        return jnp.broadcast_to(x, (blk, HEAD_DIM))

    for g, (_, d) in enumerate(ATTN_GROUPS):
        for r in range(d):
            for nb in range(ATT_TILE // d // blk):
                def rows_of(block):
                    start = r + d * blk * block
                    return pl.ds(start, blk, stride=d) if d > 1 else pl.ds(start, blk)

                rows = rows_of(nb)
                q = q_refs[g][rows, :].astype(BF16)
                if nb == 0:
                    kp, vp, pmask = kh_refs[g][rows_of(0), :], vh_refs[g][rows_of(0), :], prev_ok_first
                else:
                    kp, vp, pmask = kc_refs[g][rows_of(nb - 1), :], vc_refs[g][rows_of(nb - 1), :], prev_ok
                kc, vc = kc_refs[g][rows, :], vc_refs[g][rows, :]
                sp = jnp.where(pmask, nt_dot(q, kp.astype(BF16)) * scale, MASKED)
                sc = jnp.where(cur_ok, nt_dot(q, kc.astype(BF16)) * scale, MASKED)
                m = jnp.maximum(jnp.max(sp, axis=-1, keepdims=True), jnp.max(sc, axis=-1, keepdims=True))
                ep = jnp.exp(sp - m)
                ec = jnp.exp(sc - m)
                den = jnp.sum(ep, axis=-1, keepdims=True) + jnp.sum(ec, axis=-1, keepdims=True)
                num = (jnp.dot(ep.astype(BF16), vp.astype(BF16), preferred_element_type=F32)
                       + jnp.dot(ec.astype(BF16), vc.astype(BF16), preferred_element_type=F32))
                if g == 0:
                    num_ref[rows, :] = num
                    m_ref[rows, :] = wide(m)
                    l_ref[rows, :] = wide(den)
                else:
                    m_old = m_ref[rows, :]
                    m_new = jnp.maximum(m_old, m)
                    a_old = jnp.exp(m_old - m_new)
                    a_new = jnp.exp(m - m_new)
                    num_ref[rows, :] = a_old * num_ref[rows, :] + a_new * num
                    l_ref[rows, :] = a_old * l_ref[rows, :] + a_new * den
                    m_ref[rows, :] = m_new
    o_ref[...] = (num_ref[...] / l_ref[...]).astype(o_ref.dtype)


def _attn_prompt(qk, z, seq, k_col_block, v_col_block):
    tile = ATT_TILE
    in_specs, args = [], []

    def cur(src, col0):
        for g in range(N_GROUPS):
            in_specs.append(pl.BlockSpec((tile, HEAD_DIM),
                                         lambda h, i, c=col0 + g * HEADS_PER_GROUP: (i, c + h)))
            args.append(src)

    def halo(src, col0):
        for g, (w, _) in enumerate(ATTN_GROUPS):
            in_specs.append(pl.BlockSpec((w, HEAD_DIM),
                                         lambda h, i, c=col0 + g * HEADS_PER_GROUP, per=tile // w:
                                         (jnp.maximum(i * per - 1, 0), c + h)))
            args.append(src)

    cur(qk, 0)
    cur(qk, k_col_block)
    halo(qk, k_col_block)
    cur(z, v_col_block)
    halo(z, v_col_block)
    return pl.pallas_call(
        _attn_prompt_kernel,
        grid=(HEADS_PER_GROUP, seq // tile),
        in_specs=in_specs,
        out_specs=pl.BlockSpec((tile, HEAD_DIM), lambda h, i: (i, h)),
        out_shape=jax.ShapeDtypeStruct((seq, HEADS_PER_GROUP * HEAD_DIM), BF16),
        scratch_shapes=[pltpu.VMEM((tile, HEAD_DIM), F32)] * 3,
        compiler_params=_params(("parallel", "arbitrary"), 48),
        name="attn_prompt",
    )(*args)


def _attn_sample_kernel(qa_ref, qd_ref, kn_ref, vn_ref, k1_ref, v1_ref, k4_ref, v4_ref, k16_ref, v16_ref, o_ref):
    bt = o_ref.shape[0]
    steps = qd_ref.shape[1]
    slab = steps * HEADS_PER_GROUP
    scale = HEAD_DIM ** -0.5
    cache_refs = ((k1_ref, v1_ref), (k4_ref, v4_ref), (k16_ref, v16_ref))
    srow = lax.broadcasted_iota(jnp.int32, (slab, 1), 0)

    def lanesum(x):
        return jnp.sum(x, axis=-1, keepdims=True)

    def fold(x, op):
        wide = jnp.broadcast_to(x, (slab, HEAD_DIM))
        half = op(wide[0:slab // 2], wide[slab // 2:slab])
        half = op(half, pltpu.roll(half, HEADS_PER_GROUP, 0))
        full = jnp.concatenate([half, half], axis=0)
        return full[:, 0:x.shape[-1]]

    def per_sequence(b, carry):
        m_g, l_g, acc_g = [], [], []
        for g, (w, d) in enumerate(ATTN_GROUPS):
            k_ref, v_ref = cache_refs[g]
            kn = kn_ref[b, g]
            vn = vn_ref[b, g]
            if d % steps == 0:
                qq = qa_ref[b, g]
                s = lanesum(k_ref[b] * qq[None]) * scale
                s_new = lanesum(kn * qq) * scale
                m = jnp.maximum(jnp.max(s, axis=0), s_new)
                e = jnp.exp(s - m[None])
                e_new = jnp.exp(s_new - m)
                l_g.append(jnp.sum(e, axis=0) + e_new)
                acc_g.append(jnp.sum(e * v_ref[b], axis=0) + e_new * vn)
                m_g.append(m)
            else:
                assert d == 1 and w == k_ref.shape[1] * steps
                slabs = k_ref.shape[1]
                pos = (steps * lax.broadcasted_iota(jnp.int32, (slabs, slab, 1), 0)
                       + lax.broadcasted_iota(jnp.int32, (slabs, slab, 1), 1) // HEADS_PER_GROUP)
                m_all = jnp.zeros((slab, 1), F32)
                l_all = jnp.zeros((slab, 1), F32)
                acc_all = jnp.zeros((slab, HEAD_DIM), F32)
                for i in range(steps):
                    qq = qd_ref[b, i]
                    s = jnp.where(pos >= i, lanesum(k_ref[b] * qq[None]) * scale, MASKED)
                    s_new = jnp.where(srow // HEADS_PER_GROUP <= i, lanesum(kn * qq) * scale, MASKED)
                    m = fold(jnp.maximum(jnp.max(s, axis=0), s_new), jnp.maximum)
                    e = jnp.exp(s - m[None])
                    e_new = jnp.exp(s_new - m)
                    l = fold(jnp.sum(e, axis=0) + e_new, jnp.add)
                    acc = fold(jnp.sum(e * v_ref[b], axis=0) + e_new * vn, jnp.add)
                    mine = (srow // HEADS_PER_GROUP) == i
                    m_all = jnp.where(mine, m, m_all)
                    l_all = jnp.where(mine, l, l_all)
                    acc_all = jnp.where(mine, acc, acc_all)
                m_g.append(m_all)
                l_g.append(l_all)
                acc_g.append(acc_all)
        m = jnp.maximum(jnp.maximum(m_g[0], m_g[1]), m_g[2])
        num = jnp.zeros((slab, HEAD_DIM), F32)
        den = jnp.zeros((slab, 1), F32)
        for g in range(N_GROUPS):
            wg = jnp.exp(m_g[g] - m)
            num = num + wg * acc_g[g]
            den = den + wg * l_g[g]
        o_ref[b] = (num / den).astype(o_ref.dtype)
        return carry

    lax.fori_loop(0, bt, per_sequence, 0)


def _attn_sample(q_aligned, q_dense, k_new, v_new, caches_k, caches_v, layer):
    batch = q_aligned.shape[0]
    steps = q_dense.shape[1]
    slab = steps * HEADS_PER_GROUP
    bt = SAMPLE_ATTN_TILE
    small = lambda n: pl.BlockSpec((bt, n, slab, HEAD_DIM), lambda i: (i, 0, 0, 0))
    in_specs = [small(N_GROUPS), small(steps), small(N_GROUPS), small(N_GROUPS)]
    args = [q_aligned, q_dense, k_new, v_new]
    for g, (w, d) in enumerate(ATTN_GROUPS):
        for cache in (caches_k[g], caches_v[g]):
            depth, cb, length, heads, hd = cache.shape
            assert cb == batch and length == w and heads == HEADS_PER_GROUP and hd == HEAD_DIM
            per_slab = max(d, steps)
            view = cache.reshape(depth, batch, length // per_slab, per_slab * heads, hd)
            args.append(view)
            in_specs.append(pl.BlockSpec((None, bt, length // per_slab, slab, hd),
                                         lambda i: (layer, i, 0, 0, 0)))
    return pl.pallas_call(
        _attn_sample_kernel,
        grid=(batch // bt,),
        in_specs=in_specs,
        out_specs=pl.BlockSpec((bt, slab, HEAD_DIM), lambda i: (i, 0, 0)),
        out_shape=jax.ShapeDtypeStruct((batch, slab, HEAD_DIM), BF16),
        compiler_params=_params(("parallel",), 48),
        name="attn_sample",
    )(*args)


def _rope_tables(seq, batch, steps):
    half = ROT_DIM // 2
    inv = ROPE_THETA ** (-jnp.arange(half, dtype=F32) / half)
    pos = jnp.concatenate([jnp.arange(seq), jnp.tile(PAST_LEN + jnp.arange(steps), batch)])
    ang = pos.astype(F32)[:, None] * inv[None, :]
    cos, sin = jnp.cos(ang), jnp.sin(ang)
    rest = HEAD_DIM - ROT_DIM
    n = pos.shape[0]
    cos_t = jnp.concatenate([cos, cos, jnp.ones((n, rest), F32)], axis=1)
    sin_t = jnp.concatenate([-sin, sin, jnp.zeros((n, rest), F32)], axis=1)
    return cos_t, sin_t


def kernel(x_prompt, x_sample, cache_k_d1, cache_v_d1, cache_k_d4, cache_v_d4, cache_k_d16, cache_v_d16,
           state_pool, state_conv, ffn1_norm, ffn1_wg, ffn1_wu, ffn1_wd, mix_norm, w_in, b_gate, pool_w,
           pool_scale, conv_w, conv_b, conv_ln_g, conv_ln_b, q_norm, k_norm, proj_pool, proj_conv, proj_attn,
           w_out, ffn2_norm, ffn2_wg, ffn2_wu, ffn2_wd):
    pb, seq, dm = x_prompt.shape
    batch, steps, _ = x_sample.shape
    depth = w_in.shape[0]
    c_pool = pool_scale.shape[-1]
    c_conv = conv_w.shape[-1]
    c_attn = N_HEADS * HEAD_DIM
    grp = HEADS_PER_GROUP * HEAD_DIM
    assert pb == 1 and steps * HEADS_PER_GROUP == 16
    assert all(w == KEYS_PER_QUERY * d for w, d in ATTN_GROUPS)
    assert c_pool == c_conv
    n_prompt = pb * seq
    n_sample = batch * steps
    col_a = c_pool
    col_q = c_pool + 2 * c_conv
    col_v = col_q + 2 * c_attn
    col_gate = col_v + c_attn
    assert col_q % (2 * c_attn) == 0 and col_gate % MIX_TILE == 0

    caches_k = (cache_k_d1, cache_k_d4, cache_k_d16)
    caches_v = (cache_v_d1, cache_v_d4, cache_v_d16)
    bf = lambda t: t.astype(BF16)
    row = lambda t: t.reshape(depth, 1, t.shape[-1])
    ffn1 = (row(ffn1_norm), bf(ffn1_wg), bf(ffn1_wu), bf(ffn1_wd))
    ffn2 = (row(ffn2_norm), bf(ffn2_wg), bf(ffn2_wu), bf(ffn2_wd))
    w_in_b, pool_w_b = bf(w_in), bf(pool_w)
    pp_b, pc_b, pa_b, wo_b = bf(proj_pool), bf(proj_conv), bf(proj_attn), bf(w_out)
    mix_gain, pool_scale_r = row(mix_norm), row(pool_scale)
    conv_b_r, ln_g_r, ln_b_r = row(conv_b), row(conv_ln_g), row(conv_ln_b)
    b_gate_r = b_gate.reshape(depth, 3, dm)
    cos_t, sin_t = _rope_tables(seq, batch, steps)

    x = jnp.concatenate([x_prompt.reshape(n_prompt, dm), x_sample.reshape(n_sample, dm)], axis=0)
    tm_major = lambda t: jnp.transpose(t, (1, 0, 2))

    outs = {name: [] for name in ("pk", "pv", "ppool", "pconv", "sk", "sv", "spool", "sconv")}
    for l in range(depth):
        x = _ffn(x, *ffn1, l)
        z = _inproj(x, mix_gain, w_in_b, l)
        qk = _qknorm(z, jnp.stack([q_norm[l], k_norm[l]]), cos_t, sin_t, col_q // (2 * c_attn))

        pool_p = _pool_prompt(z, seq, pool_w_b, pool_scale_r, l)
        zp_s = z[n_prompt:, :c_pool].reshape(batch, steps, c_pool)
        pool_ext = jnp.concatenate([state_pool[l], zp_s], axis=1)
        pool_s = _pool_sample(tm_major(pool_ext), steps, pool_w_b, pool_scale_r, l)
        pool_ps = (pool_p, tm_major(pool_s).reshape(n_sample, c_pool))
        outs["ppool"].append(z[n_prompt - POOL_STATE:n_prompt, :c_pool].reshape(pb, POOL_STATE, c_pool))
        outs["spool"].append(pool_ext[:, -POOL_STATE:])

        conv_p, glu_tail = _conv_prompt(z, seq, col_a // c_conv, conv_w, conv_b_r, ln_g_r, ln_b_r, l)
        a_s = z[n_prompt:, col_a:col_a + c_conv].reshape(batch, steps, c_conv)
        b_s = z[n_prompt:, col_a + c_conv:col_a + 2 * c_conv].reshape(batch, steps, c_conv)
        conv_s, glu_s = _conv_sample(tm_major(a_s), tm_major(b_s), tm_major(state_conv[l]),
                                     conv_w, conv_b_r, ln_g_r, ln_b_r, l)
        conv_ps = (conv_p, tm_major(conv_s).reshape(n_sample, c_conv))
        outs["pconv"].append(glu_tail[CONV_HALO - CONV_STATE:].reshape(pb, CONV_STATE, c_conv))
        outs["sconv"].append(jnp.concatenate([state_conv[l], tm_major(glu_s)], axis=1)[:, -CONV_STATE:])

        attn_p = _attn_prompt(qk, z, seq, N_HEADS, col_v // HEAD_DIM)

        slab = steps * HEADS_PER_GROUP
        by_group = lambda t: jnp.transpose(t.reshape(batch, steps, N_GROUPS, HEADS_PER_GROUP, HEAD_DIM),
                                           (0, 2, 1, 3, 4))
        q_s = by_group(qk[n_prompt:, :c_attn])
        k_s = by_group(qk[n_prompt:, c_attn:])
        v_s = by_group(z[n_prompt:, col_v:col_v + c_attn])
        dense = [g for g, (_, d) in enumerate(ATTN_GROUPS) if d % steps != 0]
        assert dense == [0]
        q_dense = jnp.tile(q_s[:, 0], (1, 1, steps, 1))
        attn_s = _attn_sample(q_s.reshape(batch, N_GROUPS, slab, HEAD_DIM), q_dense,
                              k_s.reshape(batch, N_GROUPS, slab, HEAD_DIM),
                              v_s.reshape(batch, N_GROUPS, slab, HEAD_DIM), caches_k, caches_v, l)
        attn_ps = (attn_p, attn_s.reshape(n_sample, grp))
        for g, (w, d) in enumerate(ATTN_GROUPS):
            keep = min(w, seq)
            kg = qk[n_prompt - keep:n_prompt, c_attn + g * grp:c_attn + (g + 1) * grp]
            vg = z[n_prompt - keep:n_prompt, col_v + g * grp:col_v + (g + 1) * grp]
            outs["pk"].append(kg.reshape(pb, keep, HEADS_PER_GROUP, HEAD_DIM))
            outs["pv"].append(vg.reshape(pb, keep, HEADS_PER_GROUP, HEAD_DIM))
            outs["sk"].append(k_s[:, g])
            outs["sv"].append(v_s[:, g])

        x = _mixout(x, pool_ps, conv_ps, attn_ps, z, col_gate, b_gate_r, pp_b, pc_b, pa_b, wo_b, l)
        x = _ffn(x, *ffn2, l)

    def per_group(name, g):
        return jnp.stack([outs[name][l * N_GROUPS + g] for l in range(depth)], axis=0)

    st = lambda name: jnp.stack(outs[name], axis=0)
    return (x[:n_prompt].reshape(pb, seq, dm), x[n_prompt:].reshape(batch, steps, dm),
            per_group("pk", 0), per_group("pv", 0), per_group("pk", 1), per_group("pv", 1),
            per_group("pk", 2), per_group("pv", 2), st("ppool"), st("pconv"),
            per_group("sk", 0), per_group("sv", 0), per_group("sk", 1), per_group("sv", 1),
            per_group("sk", 2), per_group("sv", 2), st("spool"), st("sconv"))
```

```python
import functools

import numpy as np
import jax
import jax.numpy as jnp
from jax import lax
from jax.experimental import pallas as pl
from jax.experimental.pallas import tpu as pltpu

F32 = jnp.float32
BF16 = jnp.bfloat16

PAST_LEN = 2048
POOL_WINDOWS = (2, 4, 8, 16)
POOL_STATE = max(POOL_WINDOWS) - 1
CONV_WIDTH = 31
CONV_STATE = CONV_WIDTH - 1
HEAD_DIM = 128
ATTN_GROUPS = ((128, 1), (512, 4), (2048, 16))
N_GROUPS = len(ATTN_GROUPS)
HEADS_PER_GROUP = 4
N_HEADS = N_GROUPS * HEADS_PER_GROUP
KEYS_PER_QUERY = 128
ROT_DIM = HEAD_DIM // 4
ROPE_THETA = 500000.0
EPS = 1e-6
MASKED = -1e30

TOKEN_TILE = 512
FF_TILE = 512
IN_TILE = 1536
MIX_TILE = 512
SEQ_TILE = 512
POOL_HALO = 16
CONV_HALO = 32
ATT_TILE = 2048
ATT_BLOCK = 128
ATT_INTERLEAVE = 8
MERGE_ROWS = 256
SAMPLE_BATCH_TILE = 32
SAMPLE_ATTN_TILE = 4
CONV_ROWS = 64
MIB = 1 << 20


def _params(semantics, vmem_mib):
    return pltpu.CompilerParams(dimension_semantics=semantics, vmem_limit_bytes=vmem_mib * MIB)


def _rms(x, gain):
    return x * lax.rsqrt(jnp.mean(x * x, axis=-1, keepdims=True) + EPS) * gain


def _sigmoid(x):
    return 1.0 / (1.0 + jnp.exp(-x))


def _ffn_kernel(x_ref, gain_ref, wg_ref, wu_ref, wd_ref, o_ref, h_ref):
    @pl.when(pl.program_id(1) == 0)
    def _():
        x = x_ref[...]
        h_ref[...] = _rms(x, gain_ref[...]).astype(BF16)
        o_ref[...] = x

    h = h_ref[...]
    g = jnp.dot(h, wg_ref[...], preferred_element_type=F32)
    u = jnp.dot(h, wu_ref[...], preferred_element_type=F32)
    a = (g * _sigmoid(g) * u).astype(BF16)
    o_ref[...] += 0.5 * jnp.dot(a, wd_ref[...], preferred_element_type=F32)


def _ffn(x, gain, wg, wu, wd, layer):
    n, dm = x.shape
    dff = wg.shape[-1]
    tm, tf = TOKEN_TILE, FF_TILE
    return pl.pallas_call(
        _ffn_kernel,
        grid=(n // tm, dff // tf),
        in_specs=[
            pl.BlockSpec((tm, dm), lambda i, j: (i, 0)),
            pl.BlockSpec((None, 1, dm), lambda i, j: (layer, 0, 0)),
            pl.BlockSpec((None, dm, tf), lambda i, j: (layer, 0, j)),
            pl.BlockSpec((None, dm, tf), lambda i, j: (layer, 0, j)),
            pl.BlockSpec((None, tf, dm), lambda i, j: (layer, j, 0)),
        ],
        out_specs=pl.BlockSpec((tm, dm), lambda i, j: (i, 0)),
        out_shape=jax.ShapeDtypeStruct((n, dm), F32),
        scratch_shapes=[pltpu.VMEM((tm, dm), BF16)],
        compiler_params=_params(("parallel", "arbitrary"), 48),
        name="ffn",
    )(x, gain, wg, wu, wd)


def _inproj_kernel(x_ref, gain_ref, w_ref, z_ref, h_ref):
    @pl.when(pl.program_id(1) == 0)
    def _():
        h_ref[...] = _rms(x_ref[...], gain_ref[...]).astype(BF16)

    z_ref[...] = jnp.dot(h_ref[...], w_ref[...], preferred_element_type=F32)


def _inproj(x, gain, w, layer):
    n, dm = x.shape
    cols = w.shape[-1]
    tm, tn = TOKEN_TILE, IN_TILE
    return pl.pallas_call(
        _inproj_kernel,
        grid=(n // tm, cols // tn),
        in_specs=[
            pl.BlockSpec((tm, dm), lambda i, j: (i, 0)),
            pl.BlockSpec((None, 1, dm), lambda i, j: (layer, 0, 0)),
            pl.BlockSpec((None, dm, tn), lambda i, j: (layer, 0, j)),
        ],
        out_specs=pl.BlockSpec((tm, tn), lambda i, j: (i, j)),
        out_shape=jax.ShapeDtypeStruct((n, cols), F32),
        scratch_shapes=[pltpu.VMEM((tm, dm), BF16)],
        compiler_params=_params(("parallel", "arbitrary"), 48),
        name="inproj",
    )(x, gain, w)


def _qknorm_kernel(z_ref, gain_ref, cos_ref, sin_ref, o_ref):
    half = ROT_DIM // 2
    c = cos_ref[...]
    s = sin_ref[...]
    lane = lax.broadcasted_iota(jnp.int32, c.shape, 1)
    for hh in range(2 * N_HEADS):
        cols = slice(hh * HEAD_DIM, (hh + 1) * HEAD_DIM)
        y = _rms(z_ref[:, cols], gain_ref[pl.ds(hh // N_HEADS, 1), :])
        partner = jnp.where(lane < half, pltpu.roll(y, HEAD_DIM - half, 1), pltpu.roll(y, half, 1))
        o_ref[:, cols] = y * c + partner * s


def _qknorm(z, gains, cos_t, sin_t, qk_col_block):
    n = z.shape[0]
    width = 2 * N_HEADS * HEAD_DIM
    tm = TOKEN_TILE
    return pl.pallas_call(
        _qknorm_kernel,
        grid=(n // tm,),
        in_specs=[
            pl.BlockSpec((tm, width), lambda i: (i, qk_col_block)),
            pl.BlockSpec((2, HEAD_DIM), lambda i: (0, 0)),
            pl.BlockSpec((tm, HEAD_DIM), lambda i: (i, 0)),
            pl.BlockSpec((tm, HEAD_DIM), lambda i: (i, 0)),
        ],
        out_specs=pl.BlockSpec((tm, width), lambda i: (i, 0)),
        out_shape=jax.ShapeDtypeStruct((n, width), F32),
        compiler_params=_params(("parallel",), 40),
        name="qknorm_rope",
    )(z, gains, cos_t, sin_t)


def _mixout_kernel(x_ref, pool_p_ref, pool_s_ref, conv_p_ref, conv_s_ref, attn_p_ref, attn_s_ref,
                   zg0_ref, zg1_ref, zg2_ref, bg_ref, pp_ref, pc_ref, pa_ref, wo_ref, o_ref, *, prompt_tiles):
    @pl.when(pl.program_id(1) == 0)
    def _():
        o_ref[...] = x_ref[...]

    is_prompt = pl.program_id(0) < prompt_tiles

    def gated(zg_ref, b, p_ref, s_ref, w_ref):
        branch = jnp.where(is_prompt, p_ref[...], s_ref[...])
        gate = _sigmoid(zg_ref[...] + bg_ref[pl.ds(b, 1), :])
        return gate * jnp.dot(branch, w_ref[...], preferred_element_type=F32)

    merged = (gated(zg0_ref, 0, pool_p_ref, pool_s_ref, pp_ref)
              + gated(zg1_ref, 1, conv_p_ref, conv_s_ref, pc_ref)
              + gated(zg2_ref, 2, attn_p_ref, attn_s_ref, pa_ref))
    o_ref[...] += jnp.dot(merged.astype(BF16), wo_ref[...], preferred_element_type=F32)


def _mixout(x, pool_ps, conv_ps, attn_ps, z, gate_col0, b_gate, proj_pool, proj_conv, proj_attn, w_out, layer):
    n, dm = x.shape
    tm, tn = TOKEN_TILE, MIX_TILE
    g0 = gate_col0 // tn
    per = dm // tn
    prompt_tiles = pool_ps[0].shape[0] // tm
    assert all(p.shape[0] == prompt_tiles * tm and s.shape[0] == tm for p, s in (pool_ps, conv_ps, attn_ps))
    assert n == (prompt_tiles + 1) * tm

    def zg_spec(b):
        return pl.BlockSpec((tm, tn), lambda i, j: (i, g0 + b * per + j))

    def branch_specs(pair):
        width = pair[0].shape[1]
        return [pl.BlockSpec((tm, width), lambda i, j: (jnp.minimum(i, prompt_tiles - 1), 0)),
                pl.BlockSpec((tm, width), lambda i, j: (0, 0))]

    return pl.pallas_call(
        functools.partial(_mixout_kernel, prompt_tiles=prompt_tiles),
        grid=(n // tm, per),
        in_specs=[
            pl.BlockSpec((tm, dm), lambda i, j: (i, 0)),
            *branch_specs(pool_ps), *branch_specs(conv_ps), *branch_specs(attn_ps),
            zg_spec(0), zg_spec(1), zg_spec(2),
            pl.BlockSpec((None, 3, tn), lambda i, j: (layer, 0, j)),
            pl.BlockSpec((None, proj_pool.shape[1], tn), lambda i, j: (layer, 0, j)),
            pl.BlockSpec((None, proj_conv.shape[1], tn), lambda i, j: (layer, 0, j)),
            pl.BlockSpec((None, proj_attn.shape[1], tn), lambda i, j: (layer, 0, j)),
            pl.BlockSpec((None, tn, dm), lambda i, j: (layer, j, 0)),
        ],
        out_specs=pl.BlockSpec((tm, dm), lambda i, j: (i, 0)),
        out_shape=jax.ShapeDtypeStruct((n, dm), F32),
        compiler_params=_params(("parallel", "arbitrary"), 48),
        name="mixout",
    )(x, *pool_ps, *conv_ps, *attn_ps, z, z, z, b_gate, proj_pool, proj_conv, proj_attn, w_out)


def _pool_prompt_kernel(cur_ref, halo_ref, w_ref, scale_ref, o_ref, ext_ref):
    tt = cur_ref.shape[0]
    i = pl.program_id(0)
    ext_ref[0:POOL_HALO, :] = jnp.where(i > 0, halo_ref[...], 0.0)
    ext_ref[POOL_HALO:, :] = cur_ref[...]
    pos = i * tt + lax.broadcasted_iota(jnp.int32, (tt, 1), 0)
    cg = cur_ref.shape[1] // len(POOL_WINDOWS)
    for g, w in enumerate(POOL_WINDOWS):
        cols = slice(g * cg, (g + 1) * cg)
        u = cur_ref[:, cols]
        win = u
        for j in range(1, w):
            win = win + ext_ref[pl.ds(POOL_HALO - j, tt), cols]
        cnt = jnp.minimum(pos + 1, w).astype(F32)
        d = (win / cnt - u).astype(BF16)
        y = jnp.dot(d, w_ref[g], preferred_element_type=F32)
        o_ref[:, cols] = (y * scale_ref[:, cols]).astype(o_ref.dtype)


def _pool_prompt(z, seq, pool_w, pool_scale, layer):
    c_pool = pool_scale.shape[-1]
    tt = SEQ_TILE
    hb = tt // POOL_HALO
    ng, cg = pool_w.shape[1], pool_w.shape[2]
    return pl.pallas_call(
        _pool_prompt_kernel,
        grid=(seq // tt,),
        in_specs=[
            pl.BlockSpec((tt, c_pool), lambda i: (i, 0)),
            pl.BlockSpec((POOL_HALO, c_pool), lambda i: (jnp.maximum(i * hb - 1, 0), 0)),
            pl.BlockSpec((None, ng, cg, cg), lambda i: (layer, 0, 0, 0)),
            pl.BlockSpec((None, 1, c_pool), lambda i: (layer, 0, 0)),
        ],
        out_specs=pl.BlockSpec((tt, c_pool), lambda i: (i, 0)),
        out_shape=jax.ShapeDtypeStruct((seq, c_pool), BF16),
        scratch_shapes=[pltpu.VMEM((tt + POOL_HALO, c_pool), F32)],
        compiler_params=_params(("parallel",), 32),
        name="pool_prompt",
    )(z, z, pool_w, pool_scale)


def _pool_sample_kernel(ext_ref, w_ref, scale_ref, o_ref):
    steps = o_ref.shape[0]
    cg = ext_ref.shape[2] // len(POOL_WINDOWS)
    for g, w in enumerate(POOL_WINDOWS):
        cols = slice(g * cg, (g + 1) * cg)
        for t in range(steps):
            u = ext_ref[POOL_STATE + t, :, cols]
            win = u
            for j in range(1, w):
                win = win + ext_ref[POOL_STATE + t - j, :, cols]
            cnt = float(min(PAST_LEN + t + 1, w))
            d = (win / cnt - u).astype(BF16)
            y = jnp.dot(d, w_ref[g], preferred_element_type=F32)
            o_ref[t, :, cols] = (y * scale_ref[:, cols]).astype(o_ref.dtype)


def _pool_sample(ext, steps, pool_w, pool_scale, layer):
    rows, batch, c_pool = ext.shape
    bt = SAMPLE_BATCH_TILE
    ng, cg = pool_w.shape[1], pool_w.shape[2]
    return pl.pallas_call(
        _pool_sample_kernel,
        grid=(batch // bt,),
        in_specs=[
            pl.BlockSpec((rows, bt, c_pool), lambda i: (0, i, 0)),
            pl.BlockSpec((None, ng, cg, cg), lambda i: (layer, 0, 0, 0)),
            pl.BlockSpec((None, 1, c_pool), lambda i: (layer, 0, 0)),
        ],
        out_specs=pl.BlockSpec((steps, bt, c_pool), lambda i: (0, i, 0)),
        out_shape=jax.ShapeDtypeStruct((steps, batch, c_pool), BF16),
        compiler_params=_params(("parallel",), 32),
        name="pool_sample",
    )(ext, pool_w, pool_scale)


def _layernorm_swish(cf, ln_g, ln_b):
    mu = jnp.mean(cf, axis=-1, keepdims=True)
    dev = cf - mu
    var = jnp.mean(dev * dev, axis=-1, keepdims=True)
    cn = dev * lax.rsqrt(var + EPS) * ln_g + ln_b
    return cn * _sigmoid(cn)


def _conv_prompt_kernel(a_ref, b_ref, ha_ref, hb_ref, w_ref, cb_ref, lg_ref, lb_ref,
                        o_ref, tail_ref, ext_ref, cf_ref):
    tt, ch = a_ref.shape
    i = pl.program_id(0)
    halo = ha_ref[...] * _sigmoid(hb_ref[...])
    ext_ref[0:CONV_HALO, :] = jnp.where(i > 0, halo, 0.0)
    ext_ref[CONV_HALO:, :] = a_ref[...] * _sigmoid(b_ref[...])
    first = CONV_HALO - CONV_STATE
    lanes = 128

    sub = 8

    def col_block(cb, carry):
        cols = pl.ds(pl.multiple_of(cb * lanes, lanes), lanes)
        for r0 in range(0, tt, CONV_ROWS):
            acc = None
            for rem in range(sub):
                rows = CONV_ROWS + (sub if rem else 0)
                part = None
                for off in range(rem, first + CONV_WIDTH, sub):
                    if off < first:
                        continue
                    term = ext_ref[pl.ds(r0 + off - rem, rows), cols] * w_ref[pl.ds(off - first, 1), cols]
                    part = term if part is None else part + term
                part = part[rem:rem + CONV_ROWS]
                acc = part if acc is None else acc + part
            cf_ref[pl.ds(r0, CONV_ROWS), cols] = acc + cb_ref[:, cols]
        return carry

    lax.fori_loop(0, ch // lanes, col_block, 0)
    o_ref[...] = _layernorm_swish(cf_ref[...], lg_ref[...], lb_ref[...]).astype(o_ref.dtype)
    tail_ref[...] = ext_ref[pl.ds(tt, CONV_HALO), :]


def _conv_prompt(z, seq, a_col_block, conv_w, conv_b, ln_g, ln_b, layer):
    ch = conv_w.shape[-1]
    tt = SEQ_TILE
    hb = tt // CONV_HALO

    def vec_spec():
        return pl.BlockSpec((None, 1, ch), lambda i: (layer, 0, 0))

    return pl.pallas_call(
        _conv_prompt_kernel,
        grid=(seq // tt,),
        in_specs=[
            pl.BlockSpec((tt, ch), lambda i: (i, a_col_block)),
            pl.BlockSpec((tt, ch), lambda i: (i, a_col_block + 1)),
            pl.BlockSpec((CONV_HALO, ch), lambda i: (jnp.maximum(i * hb - 1, 0), a_col_block)),
            pl.BlockSpec((CONV_HALO, ch), lambda i: (jnp.maximum(i * hb - 1, 0), a_col_block + 1)),
            pl.BlockSpec((None, CONV_WIDTH, ch), lambda i: (layer, 0, 0)),
            vec_spec(), vec_spec(), vec_spec(),
        ],
        out_specs=[
            pl.BlockSpec((tt, ch), lambda i: (i, 0)),
            pl.BlockSpec((CONV_HALO, ch), lambda i: (0, 0)),
        ],
        out_shape=[
            jax.ShapeDtypeStruct((seq, ch), BF16),
            jax.ShapeDtypeStruct((CONV_HALO, ch), F32),
        ],
        scratch_shapes=[pltpu.VMEM((tt + CONV_HALO, ch), F32), pltpu.VMEM((tt, ch), F32)],
        compiler_params=_params(("arbitrary",), 40),
        name="conv_prompt",
    )(z, z, z, z, conv_w, conv_b, ln_g, ln_b)


def _conv_sample_kernel(a_ref, b_ref, st_ref, w_ref, cb_ref, lg_ref, lb_ref, o_ref, glu_ref, cf_ref):
    steps, bt, ch = a_ref.shape
    glu_ref[...] = a_ref[...] * _sigmoid(b_ref[...])
    lanes = 256

    def ext_row(r, cols):
        if r < CONV_STATE:
            return st_ref[r, :, cols]
        return glu_ref[r - CONV_STATE, :, cols]

    for t in range(steps):
        for c0 in range(0, ch, lanes):
            cols = slice(c0, c0 + lanes)
            acc = jnp.zeros((bt, lanes), F32)
            for j in range(CONV_WIDTH):
                acc = acc + ext_row(t + j, cols) * w_ref[pl.ds(j, 1), cols]
            cf_ref[:, cols] = acc + cb_ref[:, cols]
        o_ref[t] = _layernorm_swish(cf_ref[...], lg_ref[...], lb_ref[...]).astype(o_ref.dtype)


def _conv_sample(a_t, b_t, state_t, conv_w, conv_b, ln_g, ln_b, layer):
    steps, batch, ch = a_t.shape
    bt = SAMPLE_BATCH_TILE

    def vec_spec():
        return pl.BlockSpec((None, 1, ch), lambda i: (layer, 0, 0))

    return pl.pallas_call(
        _conv_sample_kernel,
        grid=(batch // bt,),
        in_specs=[
            pl.BlockSpec((steps, bt, ch), lambda i: (0, i, 0)),
            pl.BlockSpec((steps, bt, ch), lambda i: (0, i, 0)),
            pl.BlockSpec((CONV_STATE, bt, ch), lambda i: (0, i, 0)),
            pl.BlockSpec((None, CONV_WIDTH, ch), lambda i: (layer, 0, 0)),
            vec_spec(), vec_spec(), vec_spec(),
        ],
        out_specs=[
            pl.BlockSpec((steps, bt, ch), lambda i: (0, i, 0)),
            pl.BlockSpec((steps, bt, ch), lambda i: (0, i, 0)),
        ],
        out_shape=[
            jax.ShapeDtypeStruct((steps, batch, ch), BF16),
            jax.ShapeDtypeStruct((steps, batch, ch), F32),
        ],
        scratch_shapes=[pltpu.VMEM((bt, ch), F32)],
        compiler_params=_params(("parallel",), 32),
        name="conv_sample",
    )(a_t, b_t, state_t, conv_w, conv_b, ln_g, ln_b)


def _attn_prompt_kernel(*refs):
    q_refs = refs[0:3]
    kc_refs = refs[3:6]
    kh_refs = refs[6:9]
    vc_refs = refs[9:12]
    vh_refs = refs[12:15]
    o_ref = refs[15]
    num_refs, m_refs, l_refs = refs[16:19], refs[19:22], refs[22:25]
    tile = pl.program_id(1)
    scale = HEAD_DIM ** -0.5
    blk = ATT_BLOCK
    qi = lax.broadcasted_iota(jnp.int32, (blk, blk), 0)
    kj = lax.broadcasted_iota(jnp.int32, (blk, blk), 1)
    cur_ok = kj <= qi
    prev_ok = kj >= qi
    prev_ok_first = kj >= qi + jnp.where(tile > 0, 0, 2 * blk)

    def nt_dot(a, b):
        return lax.dot_general(a, b, (((1,), (1,)), ((), ())), preferred_element_type=F32)

    def wide(x):
        return jnp.broadcast_to(x, (blk, HEAD_DIM))

    def rows_of(r, d, block):
        start = r + d * blk * block
        return pl.ds(start, blk, stride=d) if d > 1 else pl.ds(start, blk)

    def lane_max(x):
        return jnp.max(x, axis=-1, keepdims=True)

    def lane_sum(x):
        return jnp.sum(x, axis=-1, keepdims=True)

    for g, (_, d) in enumerate(ATTN_GROUPS):
        blocks = [(r, nb) for r in range(d) for nb in range(ATT_TILE // d // blk)]
        for b0 in range(0, len(blocks), ATT_INTERLEAVE):
            work = []
            for r, nb in blocks[b0:b0 + ATT_INTERLEAVE]:
                rows = rows_of(r, d, nb)
                q = q_refs[g][rows, :].astype(BF16)
                if nb == 0:
                    prev = rows_of(r, d, 0)
                    kp, vp, pmask = kh_refs[g][prev, :], vh_refs[g][prev, :], prev_ok_first
                else:
                    prev = rows_of(r, d, nb - 1)
                    kp, vp, pmask = kc_refs[g][prev, :], vc_refs[g][prev, :], prev_ok
                work.append((rows, q, kp.astype(BF16), vp.astype(BF16), pmask,
                             kc_refs[g][rows, :].astype(BF16), vc_refs[g][rows, :].astype(BF16)))
            scores = [(jnp.where(pmask, nt_dot(q, kp) * scale, MASKED), jnp.where(cur_ok, nt_dot(q, kc) * scale, MASKED))
                      for _, q, kp, _, pmask, kc, _ in work]
            maxes = [jnp.maximum(lane_max(sp), lane_max(sc)) for sp, sc in scores]
            exps = [(jnp.exp(sp - m), jnp.exp(sc - m)) for (sp, sc), m in zip(scores, maxes)]
            dens = [lane_sum(ep) + lane_sum(ec) for ep, ec in exps]
            nums = [jnp.dot(ep.astype(BF16), vp, preferred_element_type=F32)
                    + jnp.dot(ec.astype(BF16), vc, preferred_element_type=F32)
                    for (ep, ec), (_, _, _, vp, _, _, vc) in zip(exps, work)]
            for (rows, *_), m, den, num in zip(work, maxes, dens, nums):
                num_refs[g][rows, :] = num
                m_refs[g][rows, :] = wide(m)
                l_refs[g][rows, :] = wide(den)

    def merge(c, carry):
        rows = pl.ds(pl.multiple_of(c * MERGE_ROWS, MERGE_ROWS), MERGE_ROWS)
        ms = [m_refs[g][rows, :] for g in range(N_GROUPS)]
        m = functools.reduce(jnp.maximum, ms)
        ws = [jnp.exp(mg - m) for mg in ms]
        num = sum(ws[g] * num_refs[g][rows, :] for g in range(N_GROUPS))
        den = sum(ws[g] * l_refs[g][rows, :] for g in range(N_GROUPS))
        o_ref[rows, :] = (num / den).astype(o_ref.dtype)
        return carry

    lax.fori_loop(0, ATT_TILE // MERGE_ROWS, merge, 0)


def _attn_prompt(qk, z, seq, k_col_block, v_col_block):
    tile = ATT_TILE
    in_specs, args = [], []

    def cur(src, col0):
        for g in range(N_GROUPS):
            in_specs.append(pl.BlockSpec((tile, HEAD_DIM),
                                         lambda h, i, c=col0 + g * HEADS_PER_GROUP: (i, c + h)))
            args.append(src)

    def halo(src, col0):
        for g, (w, _) in enumerate(ATTN_GROUPS):
            in_specs.append(pl.BlockSpec((w, HEAD_DIM),
                                         lambda h, i, c=col0 + g * HEADS_PER_GROUP, per=tile // w:
                                         (jnp.maximum(i * per - 1, 0), c + h)))
            args.append(src)

    cur(qk, 0)
    cur(qk, k_col_block)
    halo(qk, k_col_block)
    cur(z, v_col_block)
    halo(z, v_col_block)
    return pl.pallas_call(
        _attn_prompt_kernel,
        grid=(HEADS_PER_GROUP, seq // tile),
        in_specs=in_specs,
        out_specs=pl.BlockSpec((tile, HEAD_DIM), lambda h, i: (i, h)),
        out_shape=jax.ShapeDtypeStruct((seq, HEADS_PER_GROUP * HEAD_DIM), BF16),
        scratch_shapes=[pltpu.VMEM((tile, HEAD_DIM), F32)] * (3 * N_GROUPS),
        compiler_params=_params(("parallel", "arbitrary"), 48),
        name="attn_prompt",
    )(*args)


def _attn_sample_kernel(qa_ref, qd_ref, kn_ref, vn_ref, k1_ref, v1_ref, k4_ref, v4_ref, k16_ref, v16_ref, o_ref):
    bt = o_ref.shape[0]
    steps = qd_ref.shape[1]
    slab = steps * HEADS_PER_GROUP
    scale = HEAD_DIM ** -0.5
    cache_refs = ((k1_ref, v1_ref), (k4_ref, v4_ref), (k16_ref, v16_ref))
    srow = lax.broadcasted_iota(jnp.int32, (slab, 1), 0)

    def lanesum(x):
        return jnp.sum(x, axis=-1, keepdims=True)

    def fold(x, op):
        wide = jnp.broadcast_to(x, (slab, HEAD_DIM))
        half = op(wide[0:slab // 2], wide[slab // 2:slab])
        half = op(half, pltpu.roll(half, HEADS_PER_GROUP, 0))
        full = jnp.concatenate([half, half], axis=0)
        return full[:, 0:x.shape[-1]]

    def per_sequence(b, carry):
        m_g, l_g, acc_g = [], [], []
        for g, (w, d) in enumerate(ATTN_GROUPS):
            k_ref, v_ref = cache_refs[g]
            kn = kn_ref[b, g]
            vn = vn_ref[b, g]
            if d % steps == 0:
                qq = qa_ref[b, g]
                s = lanesum(k_ref[b] * qq[None]) * scale
                s_new = lanesum(kn * qq) * scale
                m = jnp.maximum(jnp.max(s, axis=0), s_new)
                e = jnp.exp(s - m[None])
                e_new = jnp.exp(s_new - m)
                l_g.append(jnp.sum(e, axis=0) + e_new)
                acc_g.append(jnp.sum(e * v_ref[b], axis=0) + e_new * vn)
                m_g.append(m)
            else:
                assert d == 1 and w == k_ref.shape[1] * steps
                slabs = k_ref.shape[1]
                pos = (steps * lax.broadcasted_iota(jnp.int32, (slabs, slab, 1), 0)
                       + lax.broadcasted_iota(jnp.int32, (slabs, slab, 1), 1) // HEADS_PER_GROUP)
                m_all = jnp.zeros((slab, 1), F32)
                l_all = jnp.zeros((slab, 1), F32)
                acc_all = jnp.zeros((slab, HEAD_DIM), F32)
                for i in range(steps):
                    qq = qd_ref[b, i]
                    s = jnp.where(pos >= i, lanesum(k_ref[b] * qq[None]) * scale, MASKED)
                    s_new = jnp.where(srow // HEADS_PER_GROUP <= i, lanesum(kn * qq) * scale, MASKED)
                    m = fold(jnp.maximum(jnp.max(s, axis=0), s_new), jnp.maximum)
                    e = jnp.exp(s - m[None])
                    e_new = jnp.exp(s_new - m)
                    l = fold(jnp.sum(e, axis=0) + e_new, jnp.add)
                    acc = fold(jnp.sum(e * v_ref[b], axis=0) + e_new * vn, jnp.add)
                    mine = (srow // HEADS_PER_GROUP) == i
                    m_all = jnp.where(mine, m, m_all)
                    l_all = jnp.where(mine, l, l_all)
                    acc_all = jnp.where(mine, acc, acc_all)
                m_g.append(m_all)
                l_g.append(l_all)
                acc_g.append(acc_all)
        m = jnp.maximum(jnp.maximum(m_g[0], m_g[1]), m_g[2])
        num = jnp.zeros((slab, HEAD_DIM), F32)
        den = jnp.zeros((slab, 1), F32)
        for g in range(N_GROUPS):
            wg = jnp.exp(m_g[g] - m)
            num = num + wg * acc_g[g]
            den = den + wg * l_g[g]
        o_ref[b] = (num / den).astype(o_ref.dtype)
        return carry

    lax.fori_loop(0, bt, per_sequence, 0)


def _attn_sample(q_aligned, q_dense, k_new, v_new, caches_k, caches_v, layer):
    batch = q_aligned.shape[0]
    steps = q_dense.shape[1]
    slab = steps * HEADS_PER_GROUP
    bt = SAMPLE_ATTN_TILE
    small = lambda n: pl.BlockSpec((bt, n, slab, HEAD_DIM), lambda i: (i, 0, 0, 0))
    in_specs = [small(N_GROUPS), small(steps), small(N_GROUPS), small(N_GROUPS)]
    args = [q_aligned, q_dense, k_new, v_new]
    for g, (w, d) in enumerate(ATTN_GROUPS):
        for cache in (caches_k[g], caches_v[g]):
            depth, cb, length, heads, hd = cache.shape
            assert cb == batch and length == w and heads == HEADS_PER_GROUP and hd == HEAD_DIM
            per_slab = max(d, steps)
            view = cache.reshape(depth, batch, length // per_slab, per_slab * heads, hd)
            args.append(view)
            in_specs.append(pl.BlockSpec((None, bt, length // per_slab, slab, hd),
                                         lambda i: (layer, i, 0, 0, 0)))
    return pl.pallas_call(
        _attn_sample_kernel,
        grid=(batch // bt,),
        in_specs=in_specs,
        out_specs=pl.BlockSpec((bt, slab, HEAD_DIM), lambda i: (i, 0, 0)),
        out_shape=jax.ShapeDtypeStruct((batch, slab, HEAD_DIM), BF16),
        compiler_params=_params(("parallel",), 48),
        name="attn_sample",
    )(*args)


def _rope_tables(seq, batch, steps):
    half = ROT_DIM // 2
    inv = ROPE_THETA ** (-jnp.arange(half, dtype=F32) / half)
    pos = jnp.concatenate([jnp.arange(seq), jnp.tile(PAST_LEN + jnp.arange(steps), batch)])
    ang = pos.astype(F32)[:, None] * inv[None, :]
    cos, sin = jnp.cos(ang), jnp.sin(ang)
    rest = HEAD_DIM - ROT_DIM
    n = pos.shape[0]
    cos_t = jnp.concatenate([cos, cos, jnp.ones((n, rest), F32)], axis=1)
    sin_t = jnp.concatenate([-sin, sin, jnp.zeros((n, rest), F32)], axis=1)
    return cos_t, sin_t


def kernel(x_prompt, x_sample, cache_k_d1, cache_v_d1, cache_k_d4, cache_v_d4, cache_k_d16, cache_v_d16,
           state_pool, state_conv, ffn1_norm, ffn1_wg, ffn1_wu, ffn1_wd, mix_norm, w_in, b_gate, pool_w,
           pool_scale, conv_w, conv_b, conv_ln_g, conv_ln_b, q_norm, k_norm, proj_pool, proj_conv, proj_attn,
           w_out, ffn2_norm, ffn2_wg, ffn2_wu, ffn2_wd):
    pb, seq, dm = x_prompt.shape
    batch, steps, _ = x_sample.shape
    depth = w_in.shape[0]
    c_pool = pool_scale.shape[-1]
    c_conv = conv_w.shape[-1]
    c_attn = N_HEADS * HEAD_DIM
    grp = HEADS_PER_GROUP * HEAD_DIM
    assert pb == 1 and steps * HEADS_PER_GROUP == 16
    assert all(w == KEYS_PER_QUERY * d for w, d in ATTN_GROUPS)
    assert c_pool == c_conv
    n_prompt = pb * seq
    n_sample = batch * steps
    col_a = c_pool
    col_q = c_pool + 2 * c_conv
    col_v = col_q + 2 * c_attn
    col_gate = col_v + c_attn
    assert col_q % (2 * c_attn) == 0 and col_gate % MIX_TILE == 0

    caches_k = (cache_k_d1, cache_k_d4, cache_k_d16)
    caches_v = (cache_v_d1, cache_v_d4, cache_v_d16)
    bf = lambda t: t.astype(BF16)
    row = lambda t: t.reshape(depth, 1, t.shape[-1])
    ffn1 = (row(ffn1_norm), bf(ffn1_wg), bf(ffn1_wu), bf(ffn1_wd))
    ffn2 = (row(ffn2_norm), bf(ffn2_wg), bf(ffn2_wu), bf(ffn2_wd))
    w_in_b, pool_w_b = bf(w_in), bf(pool_w)
    pp_b, pc_b, pa_b, wo_b = bf(proj_pool), bf(proj_conv), bf(proj_attn), bf(w_out)
    mix_gain, pool_scale_r = row(mix_norm), row(pool_scale)
    conv_b_r, ln_g_r, ln_b_r = row(conv_b), row(conv_ln_g), row(conv_ln_b)
    b_gate_r = b_gate.reshape(depth, 3, dm)
    cos_t, sin_t = _rope_tables(seq, batch, steps)

    x = jnp.concatenate([x_prompt.reshape(n_prompt, dm), x_sample.reshape(n_sample, dm)], axis=0)
    tm_major = lambda t: jnp.transpose(t, (1, 0, 2))

    outs = {name: [] for name in ("pk", "pv", "ppool", "pconv", "sk", "sv", "spool", "sconv")}
    for l in range(depth):
        x = _ffn(x, *ffn1, l)
        z = _inproj(x, mix_gain, w_in_b, l)
        qk = _qknorm(z, jnp.stack([q_norm[l], k_norm[l]]), cos_t, sin_t, col_q // (2 * c_attn))

        pool_p = _pool_prompt(z, seq, pool_w_b, pool_scale_r, l)
        zp_s = z[n_prompt:, :c_pool].reshape(batch, steps, c_pool)
        pool_ext = jnp.concatenate([state_pool[l], zp_s], axis=1)
        pool_s = _pool_sample(tm_major(pool_ext), steps, pool_w_b, pool_scale_r, l)
        pool_ps = (pool_p, tm_major(pool_s).reshape(n_sample, c_pool))
        outs["ppool"].append(z[n_prompt - POOL_STATE:n_prompt, :c_pool].reshape(pb, POOL_STATE, c_pool))
        outs["spool"].append(pool_ext[:, -POOL_STATE:])

        conv_p, glu_tail = _conv_prompt(z, seq, col_a // c_conv, conv_w, conv_b_r, ln_g_r, ln_b_r, l)
        a_s = z[n_prompt:, col_a:col_a + c_conv].reshape(batch, steps, c_conv)
        b_s = z[n_prompt:, col_a + c_conv:col_a + 2 * c_conv].reshape(batch, steps, c_conv)
        conv_s, glu_s = _conv_sample(tm_major(a_s), tm_major(b_s), tm_major(state_conv[l]),
                                     conv_w, conv_b_r, ln_g_r, ln_b_r, l)
        conv_ps = (conv_p, tm_major(conv_s).reshape(n_sample, c_conv))
        outs["pconv"].append(glu_tail[CONV_HALO - CONV_STATE:].reshape(pb, CONV_STATE, c_conv))
        outs["sconv"].append(jnp.concatenate([state_conv[l], tm_major(glu_s)], axis=1)[:, -CONV_STATE:])

        attn_p = _attn_prompt(qk, z, seq, N_HEADS, col_v // HEAD_DIM)

        slab = steps * HEADS_PER_GROUP
        by_group = lambda t: jnp.transpose(t.reshape(batch, steps, N_GROUPS, HEADS_PER_GROUP, HEAD_DIM),
                                           (0, 2, 1, 3, 4))
        q_s = by_group(qk[n_prompt:, :c_attn])
        k_s = by_group(qk[n_prompt:, c_attn:])
        v_s = by_group(z[n_prompt:, col_v:col_v + c_attn])
        dense = [g for g, (_, d) in enumerate(ATTN_GROUPS) if d % steps != 0]
        assert dense == [0]
        q_dense = jnp.tile(q_s[:, 0], (1, 1, steps, 1))
        attn_s = _attn_sample(q_s.reshape(batch, N_GROUPS, slab, HEAD_DIM), q_dense,
                              k_s.reshape(batch, N_GROUPS, slab, HEAD_DIM),
                              v_s.reshape(batch, N_GROUPS, slab, HEAD_DIM), caches_k, caches_v, l)
        attn_ps = (attn_p, attn_s.reshape(n_sample, grp))
        for g, (w, d) in enumerate(ATTN_GROUPS):
            keep = min(w, seq)
            kg = qk[n_prompt - keep:n_prompt, c_attn + g * grp:c_attn + (g + 1) * grp]
            vg = z[n_prompt - keep:n_prompt, col_v + g * grp:col_v + (g + 1) * grp]
            outs["pk"].append(kg.reshape(pb, keep, HEADS_PER_GROUP, HEAD_DIM))
            outs["pv"].append(vg.reshape(pb, keep, HEADS_PER_GROUP, HEAD_DIM))
            outs["sk"].append(k_s[:, g])
            outs["sv"].append(v_s[:, g])

        x = _mixout(x, pool_ps, conv_ps, attn_ps, z, col_gate, b_gate_r, pp_b, pc_b, pa_b, wo_b, l)
        x = _ffn(x, *ffn2, l)

    def per_group(name, g):
        return jnp.stack([outs[name][l * N_GROUPS + g] for l in range(depth)], axis=0)

    st = lambda name: jnp.stack(outs[name], axis=0)
    return (x[:n_prompt].reshape(pb, seq, dm), x[n_prompt:].reshape(batch, steps, dm),
            per_group("pk", 0), per_group("pv", 0), per_group("pk", 1), per_group("pv", 1),
            per_group("pk", 2), per_group("pv", 2), st("ppool"), st("pconv"),
            per_group("sk", 0), per_group("sv", 0), per_group("sk", 1), per_group("sv", 1),
            per_group("sk", 2), per_group("sv", 2), st("spool"), st("sconv"))
```

```python
import functools

import numpy as np
import jax
import jax.numpy as jnp
from jax import lax
from jax.experimental import pallas as pl
from jax.experimental.pallas import tpu as pltpu

F32 = jnp.float32
BF16 = jnp.bfloat16

PAST_LEN = 2048
POOL_WINDOWS = (2, 4, 8, 16)
POOL_STATE = max(POOL_WINDOWS) - 1
CONV_WIDTH = 31
CONV_STATE = CONV_WIDTH - 1
HEAD_DIM = 128
ATTN_GROUPS = ((128, 1), (512, 4), (2048, 16))
N_GROUPS = len(ATTN_GROUPS)
HEADS_PER_GROUP = 4
N_HEADS = N_GROUPS * HEADS_PER_GROUP
KEYS_PER_QUERY = 128
ROT_DIM = HEAD_DIM // 4
ROPE_THETA = 500000.0
EPS = 1e-6
MASKED = -1e30

TOKEN_TILE = 512
FF_TILE = 512
IN_TILE = 768
IN_TOKEN_TILE = 1088
MIX_TILE = 512
SEQ_TILE = 512
POOL_HALO = 16
CONV_HALO = 32
ATT_TILE = 2048
ATT_BLOCK = 128
ATT_INTERLEAVE = 8
MERGE_ROWS = 256
SAMPLE_BATCH_TILE = 32
SAMPLE_ATTN_TILE = 4
CONV_ROWS = 64
MIB = 1 << 20


def _params(semantics, vmem_mib):
    return pltpu.CompilerParams(dimension_semantics=semantics, vmem_limit_bytes=vmem_mib * MIB)


def _rms(x, gain):
    return x * lax.rsqrt(jnp.mean(x * x, axis=-1, keepdims=True) + EPS) * gain


def _sigmoid(x):
    return 1.0 / (1.0 + jnp.exp(-x))


def _ffn_kernel(x_ref, gain_ref, wg_ref, wu_ref, wd_ref, o_ref, h_ref):
    @pl.when(pl.program_id(1) == 0)
    def _():
        x = x_ref[...]
        h_ref[...] = _rms(x, gain_ref[...]).astype(BF16)
        o_ref[...] = x

    h = h_ref[...]
    g = jnp.dot(h, wg_ref[...], preferred_element_type=F32)
    u = jnp.dot(h, wu_ref[...], preferred_element_type=F32)
    a = (g * _sigmoid(g) * u).astype(BF16)
    o_ref[...] += 0.5 * jnp.dot(a, wd_ref[...], preferred_element_type=F32)


def _ffn(x, gain, wg, wu, wd, layer):
    n, dm = x.shape
    dff = wg.shape[-1]
    tm, tf = TOKEN_TILE, FF_TILE
    return pl.pallas_call(
        _ffn_kernel,
        grid=(n // tm, dff // tf),
        in_specs=[
            pl.BlockSpec((tm, dm), lambda i, j: (i, 0)),
            pl.BlockSpec((None, 1, dm), lambda i, j: (layer, 0, 0)),
            pl.BlockSpec((None, dm, tf), lambda i, j: (layer, 0, j)),
            pl.BlockSpec((None, dm, tf), lambda i, j: (layer, 0, j)),
            pl.BlockSpec((None, tf, dm), lambda i, j: (layer, j, 0)),
        ],
        out_specs=pl.BlockSpec((tm, dm), lambda i, j: (i, 0)),
        out_shape=jax.ShapeDtypeStruct((n, dm), F32),
        scratch_shapes=[pltpu.VMEM((tm, dm), BF16)],
        compiler_params=_params(("parallel", "arbitrary"), 48),
        name="ffn",
    )(x, gain, wg, wu, wd)


def _inproj_kernel(x_ref, gain_ref, w_ref, z_ref, h_ref):
    @pl.when(pl.program_id(1) == 0)
    def _():
        h_ref[...] = _rms(x_ref[...], gain_ref[...]).astype(BF16)

    z_ref[...] = jnp.dot(h_ref[...], w_ref[...], preferred_element_type=F32)


def _inproj(x, gain, w, layer):
    n, dm = x.shape
    cols = w.shape[-1]
    tm, tn = IN_TOKEN_TILE, IN_TILE
    assert n % tm == 0 and cols % tn == 0
    return pl.pallas_call(
        _inproj_kernel,
        grid=(n // tm, cols // tn),
        in_specs=[
            pl.BlockSpec((tm, dm), lambda i, j: (i, 0)),
            pl.BlockSpec((None, 1, dm), lambda i, j: (layer, 0, 0)),
            pl.BlockSpec((None, dm, tn), lambda i, j: (layer, 0, j)),
        ],
        out_specs=pl.BlockSpec((tm, tn), lambda i, j: (i, j)),
        out_shape=jax.ShapeDtypeStruct((n, cols), F32),
        scratch_shapes=[pltpu.VMEM((tm, dm), BF16)],
        compiler_params=_params(("parallel", "arbitrary"), 48),
        name="inproj",
    )(x, gain, w)


def _qknorm_kernel(z_ref, gain_ref, cos_ref, sin_ref, o_ref):
    half = ROT_DIM // 2
    c = cos_ref[...]
    s = sin_ref[...]
    lane = lax.broadcasted_iota(jnp.int32, c.shape, 1)
    for hh in range(2 * N_HEADS):
        cols = slice(hh * HEAD_DIM, (hh + 1) * HEAD_DIM)
        y = _rms(z_ref[:, cols], gain_ref[pl.ds(hh // N_HEADS, 1), :])
        partner = jnp.where(lane < half, pltpu.roll(y, HEAD_DIM - half, 1), pltpu.roll(y, half, 1))
        o_ref[:, cols] = y * c + partner * s


def _qknorm(z, gains, cos_t, sin_t, qk_col_block):
    n = z.shape[0]
    width = 2 * N_HEADS * HEAD_DIM
    tm = TOKEN_TILE
    return pl.pallas_call(
        _qknorm_kernel,
        grid=(n // tm,),
        in_specs=[
            pl.BlockSpec((tm, width), lambda i: (i, qk_col_block)),
            pl.BlockSpec((2, HEAD_DIM), lambda i: (0, 0)),
            pl.BlockSpec((tm, HEAD_DIM), lambda i: (i, 0)),
            pl.BlockSpec((tm, HEAD_DIM), lambda i: (i, 0)),
        ],
        out_specs=pl.BlockSpec((tm, width), lambda i: (i, 0)),
        out_shape=jax.ShapeDtypeStruct((n, width), F32),
        compiler_params=_params(("parallel",), 40),
        name="qknorm_rope",
    )(z, gains, cos_t, sin_t)


def _mixout_kernel(x_ref, pool_p_ref, pool_s_ref, conv_p_ref, conv_s_ref, attn_p_ref, attn_s_ref,
                   zg0_ref, zg1_ref, zg2_ref, bg_ref, pp_ref, pc_ref, pa_ref, wo_ref, o_ref, *, prompt_tiles):
    @pl.when(pl.program_id(1) == 0)
    def _():
        o_ref[...] = x_ref[...]

    is_prompt = pl.program_id(0) < prompt_tiles

    def gated(zg_ref, b, p_ref, s_ref, w_ref):
        branch = jnp.where(is_prompt, p_ref[...], s_ref[...])
        gate = _sigmoid(zg_ref[...] + bg_ref[pl.ds(b, 1), :])
        return gate * jnp.dot(branch, w_ref[...], preferred_element_type=F32)

    merged = (gated(zg0_ref, 0, pool_p_ref, pool_s_ref, pp_ref)
              + gated(zg1_ref, 1, conv_p_ref, conv_s_ref, pc_ref)
              + gated(zg2_ref, 2, attn_p_ref, attn_s_ref, pa_ref))
    o_ref[...] += jnp.dot(merged.astype(BF16), wo_ref[...], preferred_element_type=F32)


def _mixout(x, pool_ps, conv_ps, attn_ps, z, gate_col0, b_gate, proj_pool, proj_conv, proj_attn, w_out, layer):
    n, dm = x.shape
    tm, tn = TOKEN_TILE, MIX_TILE
    g0 = gate_col0 // tn
    per = dm // tn
    prompt_tiles = pool_ps[0].shape[0] // tm
    assert all(p.shape[0] == prompt_tiles * tm and s.shape[0] == tm for p, s in (pool_ps, conv_ps, attn_ps))
    assert n == (prompt_tiles + 1) * tm

    def zg_spec(b):
        return pl.BlockSpec((tm, tn), lambda i, j: (i, g0 + b * per + j))

    def branch_specs(pair):
        width = pair[0].shape[1]
        return [pl.BlockSpec((tm, width), lambda i, j: (jnp.minimum(i, prompt_tiles - 1), 0)),
                pl.BlockSpec((tm, width), lambda i, j: (0, 0))]

    return pl.pallas_call(
        functools.partial(_mixout_kernel, prompt_tiles=prompt_tiles),
        grid=(n // tm, per),
        in_specs=[
            pl.BlockSpec((tm, dm), lambda i, j: (i, 0)),
            *branch_specs(pool_ps), *branch_specs(conv_ps), *branch_specs(attn_ps),
            zg_spec(0), zg_spec(1), zg_spec(2),
            pl.BlockSpec((None, 3, tn), lambda i, j: (layer, 0, j)),
            pl.BlockSpec((None, proj_pool.shape[1], tn), lambda i, j: (layer, 0, j)),
            pl.BlockSpec((None, proj_conv.shape[1], tn), lambda i, j: (layer, 0, j)),
            pl.BlockSpec((None, proj_attn.shape[1], tn), lambda i, j: (layer, 0, j)),
            pl.BlockSpec((None, tn, dm), lambda i, j: (layer, j, 0)),
        ],
        out_specs=pl.BlockSpec((tm, dm), lambda i, j: (i, 0)),
        out_shape=jax.ShapeDtypeStruct((n, dm), F32),
        compiler_params=_params(("parallel", "arbitrary"), 48),
        name="mixout",
    )(x, *pool_ps, *conv_ps, *attn_ps, z, z, z, b_gate, proj_pool, proj_conv, proj_attn, w_out)


def _pool_prompt_kernel(cur_ref, halo_ref, w_ref, scale_ref, o_ref, ext_ref):
    tt = cur_ref.shape[0]
    i = pl.program_id(0)
    ext_ref[0:POOL_HALO, :] = jnp.where(i > 0, halo_ref[...], 0.0)
    ext_ref[POOL_HALO:, :] = cur_ref[...]
    pos = i * tt + lax.broadcasted_iota(jnp.int32, (tt, 1), 0)
    cg = cur_ref.shape[1] // len(POOL_WINDOWS)
    for g, w in enumerate(POOL_WINDOWS):
        cols = slice(g * cg, (g + 1) * cg)
        u = cur_ref[:, cols]
        win = u
        for j in range(1, w):
            win = win + ext_ref[pl.ds(POOL_HALO - j, tt), cols]
        cnt = jnp.minimum(pos + 1, w).astype(F32)
        d = (win / cnt - u).astype(BF16)
        y = jnp.dot(d, w_ref[g], preferred_element_type=F32)
        o_ref[:, cols] = (y * scale_ref[:, cols]).astype(o_ref.dtype)


def _pool_prompt(z, seq, pool_w, pool_scale, layer):
    c_pool = pool_scale.shape[-1]
    tt = SEQ_TILE
    hb = tt // POOL_HALO
    ng, cg = pool_w.shape[1], pool_w.shape[2]
    return pl.pallas_call(
        _pool_prompt_kernel,
        grid=(seq // tt,),
        in_specs=[
            pl.BlockSpec((tt, c_pool), lambda i: (i, 0)),
            pl.BlockSpec((POOL_HALO, c_pool), lambda i: (jnp.maximum(i * hb - 1, 0), 0)),
            pl.BlockSpec((None, ng, cg, cg), lambda i: (layer, 0, 0, 0)),
            pl.BlockSpec((None, 1, c_pool), lambda i: (layer, 0, 0)),
        ],
        out_specs=pl.BlockSpec((tt, c_pool), lambda i: (i, 0)),
        out_shape=jax.ShapeDtypeStruct((seq, c_pool), BF16),
        scratch_shapes=[pltpu.VMEM((tt + POOL_HALO, c_pool), F32)],
        compiler_params=_params(("parallel",), 32),
        name="pool_prompt",
    )(z, z, pool_w, pool_scale)


def _pool_sample_kernel(ext_ref, w_ref, scale_ref, o_ref):
    steps = o_ref.shape[0]
    cg = ext_ref.shape[2] // len(POOL_WINDOWS)
    for g, w in enumerate(POOL_WINDOWS):
        cols = slice(g * cg, (g + 1) * cg)
        for t in range(steps):
            u = ext_ref[POOL_STATE + t, :, cols]
            win = u
            for j in range(1, w):
                win = win + ext_ref[POOL_STATE + t - j, :, cols]
            cnt = float(min(PAST_LEN + t + 1, w))
            d = (win / cnt - u).astype(BF16)
            y = jnp.dot(d, w_ref[g], preferred_element_type=F32)
            o_ref[t, :, cols] = (y * scale_ref[:, cols]).astype(o_ref.dtype)


def _pool_sample(ext, steps, pool_w, pool_scale, layer):
    rows, batch, c_pool = ext.shape
    bt = SAMPLE_BATCH_TILE
    ng, cg = pool_w.shape[1], pool_w.shape[2]
    return pl.pallas_call(
        _pool_sample_kernel,
        grid=(batch // bt,),
        in_specs=[
            pl.BlockSpec((rows, bt, c_pool), lambda i: (0, i, 0)),
            pl.BlockSpec((None, ng, cg, cg), lambda i: (layer, 0, 0, 0)),
            pl.BlockSpec((None, 1, c_pool), lambda i: (layer, 0, 0)),
        ],
        out_specs=pl.BlockSpec((steps, bt, c_pool), lambda i: (0, i, 0)),
        out_shape=jax.ShapeDtypeStruct((steps, batch, c_pool), BF16),
        compiler_params=_params(("parallel",), 32),
        name="pool_sample",
    )(ext, pool_w, pool_scale)


def _layernorm_swish(cf, ln_g, ln_b):
    mu = jnp.mean(cf, axis=-1, keepdims=True)
    dev = cf - mu
    var = jnp.mean(dev * dev, axis=-1, keepdims=True)
    cn = dev * lax.rsqrt(var + EPS) * ln_g + ln_b
    return cn * _sigmoid(cn)


def _conv_prompt_kernel(a_ref, b_ref, ha_ref, hb_ref, w_ref, cb_ref, lg_ref, lb_ref,
                        o_ref, tail_ref, ext_ref, cf_ref):
    tt, ch = a_ref.shape
    i = pl.program_id(0)
    halo = ha_ref[...] * _sigmoid(hb_ref[...])
    ext_ref[0:CONV_HALO, :] = jnp.where(i > 0, halo, 0.0)
    ext_ref[CONV_HALO:, :] = a_ref[...] * _sigmoid(b_ref[...])
    first = CONV_HALO - CONV_STATE
    lanes = 128

    sub = 8

    def col_block(cb, carry):
        cols = pl.ds(pl.multiple_of(cb * lanes, lanes), lanes)
        for r0 in range(0, tt, CONV_ROWS):
            acc = None
            for rem in range(sub):
                rows = CONV_ROWS + (sub if rem else 0)
                part = None
                for off in range(rem, first + CONV_WIDTH, sub):
                    if off < first:
                        continue
                    term = ext_ref[pl.ds(r0 + off - rem, rows), cols] * w_ref[pl.ds(off - first, 1), cols]
                    part = term if part is None else part + term
                part = part[rem:rem + CONV_ROWS]
                acc = part if acc is None else acc + part
            cf_ref[pl.ds(r0, CONV_ROWS), cols] = acc + cb_ref[:, cols]
        return carry

    lax.fori_loop(0, ch // lanes, col_block, 0)
    o_ref[...] = _layernorm_swish(cf_ref[...], lg_ref[...], lb_ref[...]).astype(o_ref.dtype)
    tail_ref[...] = ext_ref[pl.ds(tt, CONV_HALO), :]


def _conv_prompt(z, seq, a_col_block, conv_w, conv_b, ln_g, ln_b, layer):
    ch = conv_w.shape[-1]
    tt = SEQ_TILE
    hb = tt // CONV_HALO

    def vec_spec():
        return pl.BlockSpec((None, 1, ch), lambda i: (layer, 0, 0))

    return pl.pallas_call(
        _conv_prompt_kernel,
        grid=(seq // tt,),
        in_specs=[
            pl.BlockSpec((tt, ch), lambda i: (i, a_col_block)),
            pl.BlockSpec((tt, ch), lambda i: (i, a_col_block + 1)),
            pl.BlockSpec((CONV_HALO, ch), lambda i: (jnp.maximum(i * hb - 1, 0), a_col_block)),
            pl.BlockSpec((CONV_HALO, ch), lambda i: (jnp.maximum(i * hb - 1, 0), a_col_block + 1)),
            pl.BlockSpec((None, CONV_WIDTH, ch), lambda i: (layer, 0, 0)),
            vec_spec(), vec_spec(), vec_spec(),
        ],
        out_specs=[
            pl.BlockSpec((tt, ch), lambda i: (i, 0)),
            pl.BlockSpec((CONV_HALO, ch), lambda i: (0, 0)),
        ],
        out_shape=[
            jax.ShapeDtypeStruct((seq, ch), BF16),
            jax.ShapeDtypeStruct((CONV_HALO, ch), F32),
        ],
        scratch_shapes=[pltpu.VMEM((tt + CONV_HALO, ch), F32), pltpu.VMEM((tt, ch), F32)],
        compiler_params=_params(("arbitrary",), 40),
        name="conv_prompt",
    )(z, z, z, z, conv_w, conv_b, ln_g, ln_b)


def _conv_sample_kernel(a_ref, b_ref, st_ref, w_ref, cb_ref, lg_ref, lb_ref, o_ref, glu_ref, cf_ref):
    steps, bt, ch = a_ref.shape
    glu_ref[...] = a_ref[...] * _sigmoid(b_ref[...])
    lanes = 256

    def ext_row(r, cols):
        if r < CONV_STATE:
            return st_ref[r, :, cols]
        return glu_ref[r - CONV_STATE, :, cols]

    for t in range(steps):
        for c0 in range(0, ch, lanes):
            cols = slice(c0, c0 + lanes)
            acc = jnp.zeros((bt, lanes), F32)
            for j in range(CONV_WIDTH):
                acc = acc + ext_row(t + j, cols) * w_ref[pl.ds(j, 1), cols]
            cf_ref[:, cols] = acc + cb_ref[:, cols]
        o_ref[t] = _layernorm_swish(cf_ref[...], lg_ref[...], lb_ref[...]).astype(o_ref.dtype)


def _conv_sample(a_t, b_t, state_t, conv_w, conv_b, ln_g, ln_b, layer):
    steps, batch, ch = a_t.shape
    bt = SAMPLE_BATCH_TILE

    def vec_spec():
        return pl.BlockSpec((None, 1, ch), lambda i: (layer, 0, 0))

    return pl.pallas_call(
        _conv_sample_kernel,
        grid=(batch // bt,),
        in_specs=[
            pl.BlockSpec((steps, bt, ch), lambda i: (0, i, 0)),
            pl.BlockSpec((steps, bt, ch), lambda i: (0, i, 0)),
            pl.BlockSpec((CONV_STATE, bt, ch), lambda i: (0, i, 0)),
            pl.BlockSpec((None, CONV_WIDTH, ch), lambda i: (layer, 0, 0)),
            vec_spec(), vec_spec(), vec_spec(),
        ],
        out_specs=[
            pl.BlockSpec((steps, bt, ch), lambda i: (0, i, 0)),
            pl.BlockSpec((steps, bt, ch), lambda i: (0, i, 0)),
        ],
        out_shape=[
            jax.ShapeDtypeStruct((steps, batch, ch), BF16),
            jax.ShapeDtypeStruct((steps, batch, ch), F32),
        ],
        scratch_shapes=[pltpu.VMEM((bt, ch), F32)],
        compiler_params=_params(("parallel",), 32),
        name="conv_sample",
    )(a_t, b_t, state_t, conv_w, conv_b, ln_g, ln_b)


def _attn_prompt_kernel(*refs):
    q_refs = refs[0:3]
    kc_refs = refs[3:6]
    kh_refs = refs[6:9]
    vc_refs = refs[9:12]
    vh_refs = refs[12:15]
    o_ref = refs[15]
    num_refs, m_refs, l_refs = refs[16:19], refs[19:22], refs[22:25]
    tile = pl.program_id(1)
    scale = HEAD_DIM ** -0.5
    blk = ATT_BLOCK
    qi = lax.broadcasted_iota(jnp.int32, (blk, blk), 0)
    kj = lax.broadcasted_iota(jnp.int32, (blk, blk), 1)
    cur_ok = kj <= qi
    prev_ok = kj >= qi
    prev_ok_first = kj >= qi + jnp.where(tile > 0, 0, 2 * blk)

    def nt_dot(a, b):
        return lax.dot_general(a, b, (((1,), (1,)), ((), ())), preferred_element_type=F32)

    def wide(x):
        return jnp.broadcast_to(x, (blk, HEAD_DIM))

    def rows_of(r, d, block):
        start = r + d * blk * block
        return pl.ds(start, blk, stride=d) if d > 1 else pl.ds(start, blk)

    def lane_max(x):
        return jnp.max(x, axis=-1, keepdims=True)

    def lane_sum(x):
        return jnp.sum(x, axis=-1, keepdims=True)

    for g, (_, d) in enumerate(ATTN_GROUPS):
        blocks = [(r, nb) for r in range(d) for nb in range(ATT_TILE // d // blk)]
        for b0 in range(0, len(blocks), ATT_INTERLEAVE):
            work = []
            for r, nb in blocks[b0:b0 + ATT_INTERLEAVE]:
                rows = rows_of(r, d, nb)
                q = q_refs[g][rows, :].astype(BF16)
                if nb == 0:
                    prev = rows_of(r, d, 0)
                    kp, vp, pmask = kh_refs[g][prev, :], vh_refs[g][prev, :], prev_ok_first
                else:
                    prev = rows_of(r, d, nb - 1)
                    kp, vp, pmask = kc_refs[g][prev, :], vc_refs[g][prev, :], prev_ok
                work.append((rows, q, kp.astype(BF16), vp.astype(BF16), pmask,
                             kc_refs[g][rows, :].astype(BF16), vc_refs[g][rows, :].astype(BF16)))
            scores = [(jnp.where(pmask, nt_dot(q, kp) * scale, MASKED), jnp.where(cur_ok, nt_dot(q, kc) * scale, MASKED))
                      for _, q, kp, _, pmask, kc, _ in work]
            maxes = [jnp.maximum(lane_max(sp), lane_max(sc)) for sp, sc in scores]
            exps = [(jnp.exp(sp - m), jnp.exp(sc - m)) for (sp, sc), m in zip(scores, maxes)]
            dens = [lane_sum(ep) + lane_sum(ec) for ep, ec in exps]
            nums = [jnp.dot(ep.astype(BF16), vp, preferred_element_type=F32)
                    + jnp.dot(ec.astype(BF16), vc, preferred_element_type=F32)
                    for (ep, ec), (_, _, _, vp, _, _, vc) in zip(exps, work)]
            for (rows, *_), m, den, num in zip(work, maxes, dens, nums):
                num_refs[g][rows, :] = num
                m_refs[g][rows, :] = wide(m)
                l_refs[g][rows, :] = wide(den)

    def merge(c, carry):
        rows = pl.ds(pl.multiple_of(c * MERGE_ROWS, MERGE_ROWS), MERGE_ROWS)
        ms = [m_refs[g][rows, :] for g in range(N_GROUPS)]
        m = functools.reduce(jnp.maximum, ms)
        ws = [jnp.exp(mg - m) for mg in ms]
        num = sum(ws[g] * num_refs[g][rows, :] for g in range(N_GROUPS))
        den = sum(ws[g] * l_refs[g][rows, :] for g in range(N_GROUPS))
        o_ref[rows, :] = (num / den).astype(o_ref.dtype)
        return carry

    lax.fori_loop(0, ATT_TILE // MERGE_ROWS, merge, 0)


def _attn_prompt(qk, z, seq, k_col_block, v_col_block):
    tile = ATT_TILE
    in_specs, args = [], []

    def cur(src, col0):
        for g in range(N_GROUPS):
            in_specs.append(pl.BlockSpec((tile, HEAD_DIM),
                                         lambda h, i, c=col0 + g * HEADS_PER_GROUP: (i, c + h)))
            args.append(src)

    def halo(src, col0):
        for g, (w, _) in enumerate(ATTN_GROUPS):
            in_specs.append(pl.BlockSpec((w, HEAD_DIM),
                                         lambda h, i, c=col0 + g * HEADS_PER_GROUP, per=tile // w:
                                         (jnp.maximum(i * per - 1, 0), c + h)))
            args.append(src)

    cur(qk, 0)
    cur(qk, k_col_block)
    halo(qk, k_col_block)
    cur(z, v_col_block)
    halo(z, v_col_block)
    return pl.pallas_call(
        _attn_prompt_kernel,
        grid=(HEADS_PER_GROUP, seq // tile),
        in_specs=in_specs,
        out_specs=pl.BlockSpec((tile, HEAD_DIM), lambda h, i: (i, h)),
        out_shape=jax.ShapeDtypeStruct((seq, HEADS_PER_GROUP * HEAD_DIM), BF16),
        scratch_shapes=[pltpu.VMEM((tile, HEAD_DIM), F32)] * (3 * N_GROUPS),
        compiler_params=_params(("parallel", "arbitrary"), 48),
        name="attn_prompt",
    )(*args)


def _attn_sample_kernel(qa_ref, qd_ref, kn_ref, vn_ref, k1_ref, v1_ref, k4_ref, v4_ref, k16_ref, v16_ref, o_ref):
    bt = o_ref.shape[0]
    steps = qd_ref.shape[1]
    slab = steps * HEADS_PER_GROUP
    scale = HEAD_DIM ** -0.5
    cache_refs = ((k1_ref, v1_ref), (k4_ref, v4_ref), (k16_ref, v16_ref))
    srow = lax.broadcasted_iota(jnp.int32, (slab, 1), 0)

    def lanesum(x):
        return jnp.sum(x, axis=-1, keepdims=True)

    def fold(x, op):
        wide = jnp.broadcast_to(x, (slab, HEAD_DIM))
        half = op(wide[0:slab // 2], wide[slab // 2:slab])
        half = op(half, pltpu.roll(half, HEADS_PER_GROUP, 0))
        full = jnp.concatenate([half, half], axis=0)
        return full[:, 0:x.shape[-1]]

    def per_sequence(b, carry):
        m_g, l_g, acc_g = [], [], []
        for g, (w, d) in enumerate(ATTN_GROUPS):
            k_ref, v_ref = cache_refs[g]
            kn = kn_ref[b, g]
            vn = vn_ref[b, g]
            if d % steps == 0:
                qq = qa_ref[b, g]
                s = lanesum(k_ref[b] * qq[None]) * scale
                s_new = lanesum(kn * qq) * scale
                m = jnp.maximum(jnp.max(s, axis=0), s_new)
                e = jnp.exp(s - m[None])
                e_new = jnp.exp(s_new - m)
                l_g.append(jnp.sum(e, axis=0) + e_new)
                acc_g.append(jnp.sum(e * v_ref[b], axis=0) + e_new * vn)
                m_g.append(m)
            else:
                assert d == 1 and w == k_ref.shape[1] * steps
                slabs = k_ref.shape[1]
                pos = (steps * lax.broadcasted_iota(jnp.int32, (slabs, slab, 1), 0)
                       + lax.broadcasted_iota(jnp.int32, (slabs, slab, 1), 1) // HEADS_PER_GROUP)
                m_all = jnp.zeros((slab, 1), F32)
                l_all = jnp.zeros((slab, 1), F32)
                acc_all = jnp.zeros((slab, HEAD_DIM), F32)
                for i in range(steps):
                    qq = qd_ref[b, i]
                    s = jnp.where(pos >= i, lanesum(k_ref[b] * qq[None]) * scale, MASKED)
                    s_new = jnp.where(srow // HEADS_PER_GROUP <= i, lanesum(kn * qq) * scale, MASKED)
                    m = fold(jnp.maximum(jnp.max(s, axis=0), s_new), jnp.maximum)
                    e = jnp.exp(s - m[None])
                    e_new = jnp.exp(s_new - m)
                    l = fold(jnp.sum(e, axis=0) + e_new, jnp.add)
                    acc = fold(jnp.sum(e * v_ref[b], axis=0) + e_new * vn, jnp.add)
                    mine = (srow // HEADS_PER_GROUP) == i
                    m_all = jnp.where(mine, m, m_all)
                    l_all = jnp.where(mine, l, l_all)
                    acc_all = jnp.where(mine, acc, acc_all)
                m_g.append(m_all)
                l_g.append(l_all)
                acc_g.append(acc_all)
        m = jnp.maximum(jnp.maximum(m_g[0], m_g[1]), m_g[2])
        num = jnp.zeros((slab, HEAD_DIM), F32)
        den = jnp.zeros((slab, 1), F32)
        for g in range(N_GROUPS):
            wg = jnp.exp(m_g[g] - m)
            num = num + wg * acc_g[g]
            den = den + wg * l_g[g]
        o_ref[b] = (num / den).astype(o_ref.dtype)
        return carry

    lax.fori_loop(0, bt, per_sequence, 0)


def _attn_sample(q_aligned, q_dense, k_new, v_new, caches_k, caches_v, layer):
    batch = q_aligned.shape[0]
    steps = q_dense.shape[1]
    slab = steps * HEADS_PER_GROUP
    bt = SAMPLE_ATTN_TILE
    small = lambda n: pl.BlockSpec((bt, n, slab, HEAD_DIM), lambda i: (i, 0, 0, 0))
    in_specs = [small(N_GROUPS), small(steps), small(N_GROUPS), small(N_GROUPS)]
    args = [q_aligned, q_dense, k_new, v_new]
    for g, (w, d) in enumerate(ATTN_GROUPS):
        for cache in (caches_k[g], caches_v[g]):
            depth, cb, length, heads, hd = cache.shape
            assert cb == batch and length == w and heads == HEADS_PER_GROUP and hd == HEAD_DIM
            per_slab = max(d, steps)
            view = cache.reshape(depth, batch, length // per_slab, per_slab * heads, hd)
            args.append(view)
            in_specs.append(pl.BlockSpec((None, bt, length // per_slab, slab, hd),
                                         lambda i: (layer, i, 0, 0, 0)))
    return pl.pallas_call(
        _attn_sample_kernel,
        grid=(batch // bt,),
        in_specs=in_specs,
        out_specs=pl.BlockSpec((bt, slab, HEAD_DIM), lambda i: (i, 0, 0)),
        out_shape=jax.ShapeDtypeStruct((batch, slab, HEAD_DIM), BF16),
        compiler_params=_params(("parallel",), 48),
        name="attn_sample",
    )(*args)


def _rope_tables(seq, batch, steps):
    half = ROT_DIM // 2
    inv = ROPE_THETA ** (-jnp.arange(half, dtype=F32) / half)
    pos = jnp.concatenate([jnp.arange(seq), jnp.tile(PAST_LEN + jnp.arange(steps), batch)])
    ang = pos.astype(F32)[:, None] * inv[None, :]
    cos, sin = jnp.cos(ang), jnp.sin(ang)
    rest = HEAD_DIM - ROT_DIM
    n = pos.shape[0]
    cos_t = jnp.concatenate([cos, cos, jnp.ones((n, rest), F32)], axis=1)
    sin_t = jnp.concatenate([-sin, sin, jnp.zeros((n, rest), F32)], axis=1)
    return cos_t, sin_t


def kernel(x_prompt, x_sample, cache_k_d1, cache_v_d1, cache_k_d4, cache_v_d4, cache_k_d16, cache_v_d16,
           state_pool, state_conv, ffn1_norm, ffn1_wg, ffn1_wu, ffn1_wd, mix_norm, w_in, b_gate, pool_w,
           pool_scale, conv_w, conv_b, conv_ln_g, conv_ln_b, q_norm, k_norm, proj_pool, proj_conv, proj_attn,
           w_out, ffn2_norm, ffn2_wg, ffn2_wu, ffn2_wd):
    pb, seq, dm = x_prompt.shape
    batch, steps, _ = x_sample.shape
    depth = w_in.shape[0]
    c_pool = pool_scale.shape[-1]
    c_conv = conv_w.shape[-1]
    c_attn = N_HEADS * HEAD_DIM
    grp = HEADS_PER_GROUP * HEAD_DIM
    assert pb == 1 and steps * HEADS_PER_GROUP == 16
    assert all(w == KEYS_PER_QUERY * d for w, d in ATTN_GROUPS)
    assert c_pool == c_conv
    n_prompt = pb * seq
    n_sample = batch * steps
    col_a = c_pool
    col_q = c_pool + 2 * c_conv
    col_v = col_q + 2 * c_attn
    col_gate = col_v + c_attn
    assert col_q % (2 * c_attn) == 0 and col_gate % MIX_TILE == 0

    caches_k = (cache_k_d1, cache_k_d4, cache_k_d16)
    caches_v = (cache_v_d1, cache_v_d4, cache_v_d16)
    bf = lambda t: t.astype(BF16)
    row = lambda t: t.reshape(depth, 1, t.shape[-1])
    ffn1 = (row(ffn1_norm), bf(ffn1_wg), bf(ffn1_wu), bf(ffn1_wd))
    ffn2 = (row(ffn2_norm), bf(ffn2_wg), bf(ffn2_wu), bf(ffn2_wd))
    w_in_b, pool_w_b = bf(w_in), bf(pool_w)
    pp_b, pc_b, pa_b, wo_b = bf(proj_pool), bf(proj_conv), bf(proj_attn), bf(w_out)
    mix_gain, pool_scale_r = row(mix_norm), row(pool_scale)
    conv_b_r, ln_g_r, ln_b_r = row(conv_b), row(conv_ln_g), row(conv_ln_b)
    b_gate_r = b_gate.reshape(depth, 3, dm)
    cos_t, sin_t = _rope_tables(seq, batch, steps)

    x = jnp.concatenate([x_prompt.reshape(n_prompt, dm), x_sample.reshape(n_sample, dm)], axis=0)
    tm_major = lambda t: jnp.transpose(t, (1, 0, 2))

    outs = {name: [] for name in ("pk", "pv", "ppool", "pconv", "sk", "sv", "spool", "sconv")}
    for l in range(depth):
        x = _ffn(x, *ffn1, l)
        z = _inproj(x, mix_gain, w_in_b, l)
        qk = _qknorm(z, jnp.stack([q_norm[l], k_norm[l]]), cos_t, sin_t, col_q // (2 * c_attn))

        pool_p = _pool_prompt(z, seq, pool_w_b, pool_scale_r, l)
        zp_s = z[n_prompt:, :c_pool].reshape(batch, steps, c_pool)
        pool_ext = jnp.concatenate([state_pool[l], zp_s], axis=1)
        pool_s = _pool_sample(tm_major(pool_ext), steps, pool_w_b, pool_scale_r, l)
        pool_ps = (pool_p, tm_major(pool_s).reshape(n_sample, c_pool))
        outs["ppool"].append(z[n_prompt - POOL_STATE:n_prompt, :c_pool].reshape(pb, POOL_STATE, c_pool))
        outs["spool"].append(pool_ext[:, -POOL_STATE:])

        conv_p, glu_tail = _conv_prompt(z, seq, col_a // c_conv, conv_w, conv_b_r, ln_g_r, ln_b_r, l)
        a_s = z[n_prompt:, col_a:col_a + c_conv].reshape(batch, steps, c_conv)
        b_s = z[n_prompt:, col_a + c_conv:col_a + 2 * c_conv].reshape(batch, steps, c_conv)
        conv_s, glu_s = _conv_sample(tm_major(a_s), tm_major(b_s), tm_major(state_conv[l]),
                                     conv_w, conv_b_r, ln_g_r, ln_b_r, l)
        conv_ps = (conv_p, tm_major(conv_s).reshape(n_sample, c_conv))
        outs["pconv"].append(glu_tail[CONV_HALO - CONV_STATE:].reshape(pb, CONV_STATE, c_conv))
        outs["sconv"].append(jnp.concatenate([state_conv[l], tm_major(glu_s)], axis=1)[:, -CONV_STATE:])

        attn_p = _attn_prompt(qk, z, seq, N_HEADS, col_v // HEAD_DIM)

        slab = steps * HEADS_PER_GROUP
        by_group = lambda t: jnp.transpose(t.reshape(batch, steps, N_GROUPS, HEADS_PER_GROUP, HEAD_DIM),
                                           (0, 2, 1, 3, 4))
        q_s = by_group(qk[n_prompt:, :c_attn])
        k_s = by_group(qk[n_prompt:, c_attn:])
        v_s = by_group(z[n_prompt:, col_v:col_v + c_attn])
        dense = [g for g, (_, d) in enumerate(ATTN_GROUPS) if d % steps != 0]
        assert dense == [0]
        q_dense = jnp.tile(q_s[:, 0], (1, 1, steps, 1))
        attn_s = _attn_sample(q_s.reshape(batch, N_GROUPS, slab, HEAD_DIM), q_dense,
                              k_s.reshape(batch, N_GROUPS, slab, HEAD_DIM),
                              v_s.reshape(batch, N_GROUPS, slab, HEAD_DIM), caches_k, caches_v, l)
        attn_ps = (attn_p, attn_s.reshape(n_sample, grp))
        for g, (w, d) in enumerate(ATTN_GROUPS):
            keep = min(w, seq)
            kg = qk[n_prompt - keep:n_prompt, c_attn + g * grp:c_attn + (g + 1) * grp]
            vg = z[n_prompt - keep:n_prompt, col_v + g * grp:col_v + (g + 1) * grp]
            outs["pk"].append(kg.reshape(pb, keep, HEADS_PER_GROUP, HEAD_DIM))
            outs["pv"].append(vg.reshape(pb, keep, HEADS_PER_GROUP, HEAD_DIM))
            outs["sk"].append(k_s[:, g])
            outs["sv"].append(v_s[:, g])

        x = _mixout(x, pool_ps, conv_ps, attn_ps, z, col_gate, b_gate_r, pp_b, pc_b, pa_b, wo_b, l)
        x = _ffn(x, *ffn2, l)

    def per_group(name, g):
        return jnp.stack([outs[name][l * N_GROUPS + g] for l in range(depth)], axis=0)

    st = lambda name: jnp.stack(outs[name], axis=0)
    return (x[:n_prompt].reshape(pb, seq, dm), x[n_prompt:].reshape(batch, steps, dm),
            per_group("pk", 0), per_group("pv", 0), per_group("pk", 1), per_group("pv", 1),
            per_group("pk", 2), per_group("pv", 2), st("ppool"), st("pconv"),
            per_group("sk", 0), per_group("sv", 0), per_group("sk", 1), per_group("sv", 1),
            per_group("sk", 2), per_group("sv", 2), st("spool"), st("sconv"))
```

```python
import functools

import numpy as np
import jax
import jax.numpy as jnp
from jax import lax
from jax.experimental import pallas as pl
from jax.experimental.pallas import tpu as pltpu

F32 = jnp.float32
BF16 = jnp.bfloat16

PAST_LEN = 2048
POOL_WINDOWS = (2, 4, 8, 16)
POOL_STATE = max(POOL_WINDOWS) - 1
CONV_WIDTH = 31
CONV_STATE = CONV_WIDTH - 1
HEAD_DIM = 128
ATTN_GROUPS = ((128, 1), (512, 4), (2048, 16))
N_GROUPS = len(ATTN_GROUPS)
HEADS_PER_GROUP = 4
N_HEADS = N_GROUPS * HEADS_PER_GROUP
KEYS_PER_QUERY = 128
ROT_DIM = HEAD_DIM // 4
ROPE_THETA = 500000.0
EPS = 1e-6
MASKED = -1e30

TOKEN_TILE = 512
FF_TILE = 512
IN_TILE = 768
IN_TOKEN_TILE = 1088
MIX_TILE = 512
SEQ_TILE = 512
POOL_HALO = 16
CONV_HALO = 32
ATT_TILE = 2048
ATT_BLOCK = 128
ATT_INTERLEAVE = 8
MERGE_ROWS = 256
SAMPLE_BATCH_TILE = 32
SAMPLE_ATTN_TILE = 4
CONV_ROWS = 64
MIB = 1 << 20


def _params(semantics, vmem_mib):
    return pltpu.CompilerParams(dimension_semantics=semantics, vmem_limit_bytes=vmem_mib * MIB)


def _rms(x, gain):
    return x * lax.rsqrt(jnp.mean(x * x, axis=-1, keepdims=True) + EPS) * gain


def _sigmoid(x):
    return 1.0 / (1.0 + jnp.exp(-x))


def _ffn_kernel(*refs, prompt_tiles):
    x_refs = refs[:-6]
    gain_ref, wg_ref, wu_ref, wd_ref, o_ref, h_ref = refs[-6:]

    @pl.when(pl.program_id(1) == 0)
    def _():
        if prompt_tiles is None:
            x = x_refs[0][...]
        else:
            x = jnp.where(pl.program_id(0) < prompt_tiles, x_refs[0][...], x_refs[1][...])
        h_ref[...] = _rms(x, gain_ref[...]).astype(BF16)
        o_ref[...] = x

    h = h_ref[...]
    g = jnp.dot(h, wg_ref[...], preferred_element_type=F32)
    u = jnp.dot(h, wu_ref[...], preferred_element_type=F32)
    a = (g * _sigmoid(g) * u).astype(BF16)
    o_ref[...] += 0.5 * jnp.dot(a, wd_ref[...], preferred_element_type=F32)


def _ffn(x, gain, wg, wu, wd, layer):
    tm, tf = TOKEN_TILE, FF_TILE
    dff = wg.shape[-1]
    if isinstance(x, tuple):
        xs = x
        dm = xs[0].shape[1]
        prompt_tiles = xs[0].shape[0] // tm
        assert xs[0].shape[0] == prompt_tiles * tm and xs[1].shape[0] == tm
        n = (prompt_tiles + 1) * tm
        x_specs = [pl.BlockSpec((tm, dm), lambda i, j: (jnp.minimum(i, prompt_tiles - 1), 0)),
                   pl.BlockSpec((tm, dm), lambda i, j: (0, 0))]
    else:
        xs = (x,)
        n, dm = x.shape
        prompt_tiles = None
        x_specs = [pl.BlockSpec((tm, dm), lambda i, j: (i, 0))]
    return pl.pallas_call(
        functools.partial(_ffn_kernel, prompt_tiles=prompt_tiles),
        grid=(n // tm, dff // tf),
        in_specs=[
            *x_specs,
            pl.BlockSpec((None, 1, dm), lambda i, j: (layer, 0, 0)),
            pl.BlockSpec((None, dm, tf), lambda i, j: (layer, 0, j)),
            pl.BlockSpec((None, dm, tf), lambda i, j: (layer, 0, j)),
            pl.BlockSpec((None, tf, dm), lambda i, j: (layer, j, 0)),
        ],
        out_specs=pl.BlockSpec((tm, dm), lambda i, j: (i, 0)),
        out_shape=jax.ShapeDtypeStruct((n, dm), F32),
        scratch_shapes=[pltpu.VMEM((tm, dm), BF16)],
        compiler_params=_params(("parallel", "arbitrary"), 48),
        name="ffn",
    )(*xs, gain, wg, wu, wd)


def _inproj_kernel(x_ref, gain_ref, w_ref, z_ref, h_ref):
    @pl.when(pl.program_id(1) == 0)
    def _():
        h_ref[...] = _rms(x_ref[...], gain_ref[...]).astype(BF16)

    z_ref[...] = jnp.dot(h_ref[...], w_ref[...], preferred_element_type=F32)


def _inproj(x, gain, w, layer):
    n, dm = x.shape
    cols = w.shape[-1]
    tm, tn = IN_TOKEN_TILE, IN_TILE
    assert n % tm == 0 and cols % tn == 0
    return pl.pallas_call(
        _inproj_kernel,
        grid=(n // tm, cols // tn),
        in_specs=[
            pl.BlockSpec((tm, dm), lambda i, j: (i, 0)),
            pl.BlockSpec((None, 1, dm), lambda i, j: (layer, 0, 0)),
            pl.BlockSpec((None, dm, tn), lambda i, j: (layer, 0, j)),
        ],
        out_specs=pl.BlockSpec((tm, tn), lambda i, j: (i, j)),
        out_shape=jax.ShapeDtypeStruct((n, cols), F32),
        scratch_shapes=[pltpu.VMEM((tm, dm), BF16)],
        compiler_params=_params(("parallel", "arbitrary"), 48),
        name="inproj",
    )(x, gain, w)


def _qknorm_kernel(z_ref, gain_ref, cos_ref, sin_ref, o_ref):
    half = ROT_DIM // 2
    c = cos_ref[...]
    s = sin_ref[...]
    lane = lax.broadcasted_iota(jnp.int32, c.shape, 1)
    for hh in range(2 * N_HEADS):
        cols = slice(hh * HEAD_DIM, (hh + 1) * HEAD_DIM)
        y = _rms(z_ref[:, cols], gain_ref[pl.ds(hh // N_HEADS, 1), :])
        partner = jnp.where(lane < half, pltpu.roll(y, HEAD_DIM - half, 1), pltpu.roll(y, half, 1))
        o_ref[:, cols] = y * c + partner * s


def _qknorm(z, gains, cos_t, sin_t, qk_col_block):
    n = z.shape[0]
    width = 2 * N_HEADS * HEAD_DIM
    tm = TOKEN_TILE
    return pl.pallas_call(
        _qknorm_kernel,
        grid=(n // tm,),
        in_specs=[
            pl.BlockSpec((tm, width), lambda i: (i, qk_col_block)),
            pl.BlockSpec((2, HEAD_DIM), lambda i: (0, 0)),
            pl.BlockSpec((tm, HEAD_DIM), lambda i: (i, 0)),
            pl.BlockSpec((tm, HEAD_DIM), lambda i: (i, 0)),
        ],
        out_specs=pl.BlockSpec((tm, width), lambda i: (i, 0)),
        out_shape=jax.ShapeDtypeStruct((n, width), F32),
        compiler_params=_params(("parallel",), 40),
        name="qknorm_rope",
    )(z, gains, cos_t, sin_t)


def _mixout_kernel(x_ref, pool_p_ref, pool_s_ref, conv_p_ref, conv_s_ref, attn_p_ref, attn_s_ref,
                   zg0_ref, zg1_ref, zg2_ref, bg_ref, pp_ref, pc_ref, pa_ref, wo_ref, o_ref, *, prompt_tiles):
    @pl.when(pl.program_id(1) == 0)
    def _():
        o_ref[...] = x_ref[...]

    is_prompt = pl.program_id(0) < prompt_tiles

    def gated(zg_ref, b, p_ref, s_ref, w_ref):
        branch = jnp.where(is_prompt, p_ref[...], s_ref[...])
        gate = _sigmoid(zg_ref[...] + bg_ref[pl.ds(b, 1), :])
        return gate * jnp.dot(branch, w_ref[...], preferred_element_type=F32)

    merged = (gated(zg0_ref, 0, pool_p_ref, pool_s_ref, pp_ref)
              + gated(zg1_ref, 1, conv_p_ref, conv_s_ref, pc_ref)
              + gated(zg2_ref, 2, attn_p_ref, attn_s_ref, pa_ref))
    o_ref[...] += jnp.dot(merged.astype(BF16), wo_ref[...], preferred_element_type=F32)


def _mixout(x, pool_ps, conv_ps, attn_ps, z, gate_col0, b_gate, proj_pool, proj_conv, proj_attn, w_out, layer):
    n, dm = x.shape
    tm, tn = TOKEN_TILE, MIX_TILE
    g0 = gate_col0 // tn
    per = dm // tn
    prompt_tiles = pool_ps[0].shape[0] // tm
    assert all(p.shape[0] == prompt_tiles * tm and s.shape[0] == tm for p, s in (pool_ps, conv_ps, attn_ps))
    assert n == (prompt_tiles + 1) * tm

    def zg_spec(b):
        return pl.BlockSpec((tm, tn), lambda i, j: (i, g0 + b * per + j))

    def branch_specs(pair):
        width = pair[0].shape[1]
        return [pl.BlockSpec((tm, width), lambda i, j: (jnp.minimum(i, prompt_tiles - 1), 0)),
                pl.BlockSpec((tm, width), lambda i, j: (0, 0))]

    return pl.pallas_call(
        functools.partial(_mixout_kernel, prompt_tiles=prompt_tiles),
        grid=(n // tm, per),
        in_specs=[
            pl.BlockSpec((tm, dm), lambda i, j: (i, 0)),
            *branch_specs(pool_ps), *branch_specs(conv_ps), *branch_specs(attn_ps),
            zg_spec(0), zg_spec(1), zg_spec(2),
            pl.BlockSpec((None, 3, tn), lambda i, j: (layer, 0, j)),
            pl.BlockSpec((None, proj_pool.shape[1], tn), lambda i, j: (layer, 0, j)),
            pl.BlockSpec((None, proj_conv.shape[1], tn), lambda i, j: (layer, 0, j)),
            pl.BlockSpec((None, proj_attn.shape[1], tn), lambda i, j: (layer, 0, j)),
            pl.BlockSpec((None, tn, dm), lambda i, j: (layer, j, 0)),
        ],
        out_specs=pl.BlockSpec((tm, dm), lambda i, j: (i, 0)),
        out_shape=jax.ShapeDtypeStruct((n, dm), F32),
        compiler_params=_params(("parallel", "arbitrary"), 48),
        name="mixout",
    )(x, *pool_ps, *conv_ps, *attn_ps, z, z, z, b_gate, proj_pool, proj_conv, proj_attn, w_out)


def _pool_prompt_kernel(cur_ref, halo_ref, w_ref, scale_ref, o_ref, ext_ref):
    tt = cur_ref.shape[0]
    i = pl.program_id(0)
    ext_ref[0:POOL_HALO, :] = jnp.where(i > 0, halo_ref[...], 0.0)
    ext_ref[POOL_HALO:, :] = cur_ref[...]
    pos = i * tt + lax.broadcasted_iota(jnp.int32, (tt, 1), 0)
    cg = cur_ref.shape[1] // len(POOL_WINDOWS)
    for g, w in enumerate(POOL_WINDOWS):
        cols = slice(g * cg, (g + 1) * cg)
        u = cur_ref[:, cols]
        win = u
        for j in range(1, w):
            win = win + ext_ref[pl.ds(POOL_HALO - j, tt), cols]
        cnt = jnp.minimum(pos + 1, w).astype(F32)
        d = (win / cnt - u).astype(BF16)
        y = jnp.dot(d, w_ref[g], preferred_element_type=F32)
        o_ref[:, cols] = (y * scale_ref[:, cols]).astype(o_ref.dtype)


def _pool_prompt(z, seq, pool_w, pool_scale, layer):
    c_pool = pool_scale.shape[-1]
    tt = SEQ_TILE
    hb = tt // POOL_HALO
    ng, cg = pool_w.shape[1], pool_w.shape[2]
    return pl.pallas_call(
        _pool_prompt_kernel,
        grid=(seq // tt,),
        in_specs=[
            pl.BlockSpec((tt, c_pool), lambda i: (i, 0)),
            pl.BlockSpec((POOL_HALO, c_pool), lambda i: (jnp.maximum(i * hb - 1, 0), 0)),
            pl.BlockSpec((None, ng, cg, cg), lambda i: (layer, 0, 0, 0)),
            pl.BlockSpec((None, 1, c_pool), lambda i: (layer, 0, 0)),
        ],
        out_specs=pl.BlockSpec((tt, c_pool), lambda i: (i, 0)),
        out_shape=jax.ShapeDtypeStruct((seq, c_pool), BF16),
        scratch_shapes=[pltpu.VMEM((tt + POOL_HALO, c_pool), F32)],
        compiler_params=_params(("parallel",), 32),
        name="pool_prompt",
    )(z, z, pool_w, pool_scale)


def _pool_sample_kernel(ext_ref, w_ref, scale_ref, o_ref):
    steps = o_ref.shape[0]
    cg = ext_ref.shape[2] // len(POOL_WINDOWS)
    for g, w in enumerate(POOL_WINDOWS):
        cols = slice(g * cg, (g + 1) * cg)
        for t in range(steps):
            u = ext_ref[POOL_STATE + t, :, cols]
            win = u
            for j in range(1, w):
                win = win + ext_ref[POOL_STATE + t - j, :, cols]
            cnt = float(min(PAST_LEN + t + 1, w))
            d = (win / cnt - u).astype(BF16)
            y = jnp.dot(d, w_ref[g], preferred_element_type=F32)
            o_ref[t, :, cols] = (y * scale_ref[:, cols]).astype(o_ref.dtype)


def _pool_sample(ext, steps, pool_w, pool_scale, layer):
    rows, batch, c_pool = ext.shape
    bt = SAMPLE_BATCH_TILE
    ng, cg = pool_w.shape[1], pool_w.shape[2]
    return pl.pallas_call(
        _pool_sample_kernel,
        grid=(batch // bt,),
        in_specs=[
            pl.BlockSpec((rows, bt, c_pool), lambda i: (0, i, 0)),
            pl.BlockSpec((None, ng, cg, cg), lambda i: (layer, 0, 0, 0)),
            pl.BlockSpec((None, 1, c_pool), lambda i: (layer, 0, 0)),
        ],
        out_specs=pl.BlockSpec((steps, bt, c_pool), lambda i: (0, i, 0)),
        out_shape=jax.ShapeDtypeStruct((steps, batch, c_pool), BF16),
        compiler_params=_params(("parallel",), 32),
        name="pool_sample",
    )(ext, pool_w, pool_scale)


def _layernorm_swish(cf, ln_g, ln_b):
    mu = jnp.mean(cf, axis=-1, keepdims=True)
    dev = cf - mu
    var = jnp.mean(dev * dev, axis=-1, keepdims=True)
    cn = dev * lax.rsqrt(var + EPS) * ln_g + ln_b
    return cn * _sigmoid(cn)


def _conv_prompt_kernel(a_ref, b_ref, ha_ref, hb_ref, w_ref, cb_ref, lg_ref, lb_ref,
                        o_ref, tail_ref, ext_ref, cf_ref):
    tt, ch = a_ref.shape
    i = pl.program_id(0)
    halo = ha_ref[...] * _sigmoid(hb_ref[...])
    ext_ref[0:CONV_HALO, :] = jnp.where(i > 0, halo, 0.0)
    ext_ref[CONV_HALO:, :] = a_ref[...] * _sigmoid(b_ref[...])
    first = CONV_HALO - CONV_STATE
    lanes = 128

    sub = 8

    def col_block(cb, carry):
        cols = pl.ds(pl.multiple_of(cb * lanes, lanes), lanes)
        for r0 in range(0, tt, CONV_ROWS):
            acc = None
            for rem in range(sub):
                rows = CONV_ROWS + (sub if rem else 0)
                part = None
                for off in range(rem, first + CONV_WIDTH, sub):
                    if off < first:
                        continue
                    term = ext_ref[pl.ds(r0 + off - rem, rows), cols] * w_ref[pl.ds(off - first, 1), cols]
                    part = term if part is None else part + term
                part = part[rem:rem + CONV_ROWS]
                acc = part if acc is None else acc + part
            cf_ref[pl.ds(r0, CONV_ROWS), cols] = acc + cb_ref[:, cols]
        return carry

    lax.fori_loop(0, ch // lanes, col_block, 0)
    o_ref[...] = _layernorm_swish(cf_ref[...], lg_ref[...], lb_ref[...]).astype(o_ref.dtype)
    tail_ref[...] = ext_ref[pl.ds(tt, CONV_HALO), :]


def _conv_prompt(z, seq, a_col_block, conv_w, conv_b, ln_g, ln_b, layer):
    ch = conv_w.shape[-1]
    tt = SEQ_TILE
    hb = tt // CONV_HALO

    def vec_spec():
        return pl.BlockSpec((None, 1, ch), lambda i: (layer, 0, 0))

    return pl.pallas_call(
        _conv_prompt_kernel,
        grid=(seq // tt,),
        in_specs=[
            pl.BlockSpec((tt, ch), lambda i: (i, a_col_block)),
            pl.BlockSpec((tt, ch), lambda i: (i, a_col_block + 1)),
            pl.BlockSpec((CONV_HALO, ch), lambda i: (jnp.maximum(i * hb - 1, 0), a_col_block)),
            pl.BlockSpec((CONV_HALO, ch), lambda i: (jnp.maximum(i * hb - 1, 0), a_col_block + 1)),
            pl.BlockSpec((None, CONV_WIDTH, ch), lambda i: (layer, 0, 0)),
            vec_spec(), vec_spec(), vec_spec(),
        ],
        out_specs=[
            pl.BlockSpec((tt, ch), lambda i: (i, 0)),
            pl.BlockSpec((CONV_HALO, ch), lambda i: (0, 0)),
        ],
        out_shape=[
            jax.ShapeDtypeStruct((seq, ch), BF16),
            jax.ShapeDtypeStruct((CONV_HALO, ch), F32),
        ],
        scratch_shapes=[pltpu.VMEM((tt + CONV_HALO, ch), F32), pltpu.VMEM((tt, ch), F32)],
        compiler_params=_params(("arbitrary",), 40),
        name="conv_prompt",
    )(z, z, z, z, conv_w, conv_b, ln_g, ln_b)


def _conv_sample_kernel(a_ref, b_ref, st_ref, w_ref, cb_ref, lg_ref, lb_ref, o_ref, glu_ref, cf_ref):
    steps, bt, ch = a_ref.shape
    glu_ref[...] = a_ref[...] * _sigmoid(b_ref[...])
    lanes = 256

    def ext_row(r, cols):
        if r < CONV_STATE:
            return st_ref[r, :, cols]
        return glu_ref[r - CONV_STATE, :, cols]

    for t in range(steps):
        for c0 in range(0, ch, lanes):
            cols = slice(c0, c0 + lanes)
            acc = jnp.zeros((bt, lanes), F32)
            for j in range(CONV_WIDTH):
                acc = acc + ext_row(t + j, cols) * w_ref[pl.ds(j, 1), cols]
            cf_ref[:, cols] = acc + cb_ref[:, cols]
        o_ref[t] = _layernorm_swish(cf_ref[...], lg_ref[...], lb_ref[...]).astype(o_ref.dtype)


def _conv_sample(a_t, b_t, state_t, conv_w, conv_b, ln_g, ln_b, layer):
    steps, batch, ch = a_t.shape
    bt = SAMPLE_BATCH_TILE

    def vec_spec():
        return pl.BlockSpec((None, 1, ch), lambda i: (layer, 0, 0))

    return pl.pallas_call(
        _conv_sample_kernel,
        grid=(batch // bt,),
        in_specs=[
            pl.BlockSpec((steps, bt, ch), lambda i: (0, i, 0)),
            pl.BlockSpec((steps, bt, ch), lambda i: (0, i, 0)),
            pl.BlockSpec((CONV_STATE, bt, ch), lambda i: (0, i, 0)),
            pl.BlockSpec((None, CONV_WIDTH, ch), lambda i: (layer, 0, 0)),
            vec_spec(), vec_spec(), vec_spec(),
        ],
        out_specs=[
            pl.BlockSpec((steps, bt, ch), lambda i: (0, i, 0)),
            pl.BlockSpec((steps, bt, ch), lambda i: (0, i, 0)),
        ],
        out_shape=[
            jax.ShapeDtypeStruct((steps, batch, ch), BF16),
            jax.ShapeDtypeStruct((steps, batch, ch), F32),
        ],
        scratch_shapes=[pltpu.VMEM((bt, ch), F32)],
        compiler_params=_params(("parallel",), 32),
        name="conv_sample",
    )(a_t, b_t, state_t, conv_w, conv_b, ln_g, ln_b)


def _attn_prompt_kernel(*refs):
    q_refs = refs[0:3]
    kc_refs = refs[3:6]
    kh_refs = refs[6:9]
    vc_refs = refs[9:12]
    vh_refs = refs[12:15]
    o_ref = refs[15]
    num_refs, m_refs, l_refs = refs[16:19], refs[19:22], refs[22:25]
    tile = pl.program_id(1)
    scale = HEAD_DIM ** -0.5
    blk = ATT_BLOCK
    qi = lax.broadcasted_iota(jnp.int32, (blk, blk), 0)
    kj = lax.broadcasted_iota(jnp.int32, (blk, blk), 1)
    cur_ok = kj <= qi
    prev_ok = kj >= qi
    prev_ok_first = kj >= qi + jnp.where(tile > 0, 0, 2 * blk)

    def nt_dot(a, b):
        return lax.dot_general(a, b, (((1,), (1,)), ((), ())), preferred_element_type=F32)

    def wide(x):
        return jnp.broadcast_to(x, (blk, HEAD_DIM))

    def rows_of(r, d, block):
        start = r + d * blk * block
        return pl.ds(start, blk, stride=d) if d > 1 else pl.ds(start, blk)

    def lane_max(x):
        return jnp.max(x, axis=-1, keepdims=True)

    def lane_sum(x):
        return jnp.sum(x, axis=-1, keepdims=True)

    for g, (_, d) in enumerate(ATTN_GROUPS):
        blocks = [(r, nb) for r in range(d) for nb in range(ATT_TILE // d // blk)]
        for b0 in range(0, len(blocks), ATT_INTERLEAVE):
            work = []
            for r, nb in blocks[b0:b0 + ATT_INTERLEAVE]:
                rows = rows_of(r, d, nb)
                q = q_refs[g][rows, :].astype(BF16)
                if nb == 0:
                    prev = rows_of(r, d, 0)
                    kp, vp, pmask = kh_refs[g][prev, :], vh_refs[g][prev, :], prev_ok_first
                else:
                    prev = rows_of(r, d, nb - 1)
                    kp, vp, pmask = kc_refs[g][prev, :], vc_refs[g][prev, :], prev_ok
                work.append((rows, q, kp.astype(BF16), vp.astype(BF16), pmask,
                             kc_refs[g][rows, :].astype(BF16), vc_refs[g][rows, :].astype(BF16)))
            scores = [(jnp.where(pmask, nt_dot(q, kp) * scale, MASKED), jnp.where(cur_ok, nt_dot(q, kc) * scale, MASKED))
                      for _, q, kp, _, pmask, kc, _ in work]
            maxes = [jnp.maximum(lane_max(sp), lane_max(sc)) for sp, sc in scores]
            exps = [(jnp.exp(sp - m), jnp.exp(sc - m)) for (sp, sc), m in zip(scores, maxes)]
            dens = [lane_sum(ep) + lane_sum(ec) for ep, ec in exps]
            nums = [jnp.dot(ep.astype(BF16), vp, preferred_element_type=F32)
                    + jnp.dot(ec.astype(BF16), vc, preferred_element_type=F32)
                    for (ep, ec), (_, _, _, vp, _, _, vc) in zip(exps, work)]
            for (rows, *_), m, den, num in zip(work, maxes, dens, nums):
                num_refs[g][rows, :] = num
                m_refs[g][rows, :] = wide(m)
                l_refs[g][rows, :] = wide(den)

    def merge(c, carry):
        rows = pl.ds(pl.multiple_of(c * MERGE_ROWS, MERGE_ROWS), MERGE_ROWS)
        ms = [m_refs[g][rows, :] for g in range(N_GROUPS)]
        m = functools.reduce(jnp.maximum, ms)
        ws = [jnp.exp(mg - m) for mg in ms]
        num = sum(ws[g] * num_refs[g][rows, :] for g in range(N_GROUPS))
        den = sum(ws[g] * l_refs[g][rows, :] for g in range(N_GROUPS))
        o_ref[rows, :] = (num / den).astype(o_ref.dtype)
        return carry

    lax.fori_loop(0, ATT_TILE // MERGE_ROWS, merge, 0)


def _attn_prompt(qk, z, seq, k_col_block, v_col_block):
    tile = ATT_TILE
    in_specs, args = [], []

    def cur(src, col0):
        for g in range(N_GROUPS):
            in_specs.append(pl.BlockSpec((tile, HEAD_DIM),
                                         lambda h, i, c=col0 + g * HEADS_PER_GROUP: (i, c + h)))
            args.append(src)

    def halo(src, col0):
        for g, (w, _) in enumerate(ATTN_GROUPS):
            in_specs.append(pl.BlockSpec((w, HEAD_DIM),
                                         lambda h, i, c=col0 + g * HEADS_PER_GROUP, per=tile // w:
                                         (jnp.maximum(i * per - 1, 0), c + h)))
            args.append(src)

    cur(qk, 0)
    cur(qk, k_col_block)
    halo(qk, k_col_block)
    cur(z, v_col_block)
    halo(z, v_col_block)
    return pl.pallas_call(
        _attn_prompt_kernel,
        grid=(HEADS_PER_GROUP, seq // tile),
        in_specs=in_specs,
        out_specs=pl.BlockSpec((tile, HEAD_DIM), lambda h, i: (i, h)),
        out_shape=jax.ShapeDtypeStruct((seq, HEADS_PER_GROUP * HEAD_DIM), BF16),
        scratch_shapes=[pltpu.VMEM((tile, HEAD_DIM), F32)] * (3 * N_GROUPS),
        compiler_params=_params(("parallel", "arbitrary"), 48),
        name="attn_prompt",
    )(*args)


def _attn_sample_kernel(qa_ref, qd_ref, kn_ref, vn_ref, k1_ref, v1_ref, k4_ref, v4_ref, k16_ref, v16_ref, o_ref):
    bt = o_ref.shape[0]
    steps = qd_ref.shape[1]
    slab = steps * HEADS_PER_GROUP
    scale = HEAD_DIM ** -0.5
    cache_refs = ((k1_ref, v1_ref), (k4_ref, v4_ref), (k16_ref, v16_ref))
    srow = lax.broadcasted_iota(jnp.int32, (slab, 1), 0)

    def lanesum(x):
        return jnp.sum(x, axis=-1, keepdims=True)

    def fold(x, op):
        wide = jnp.broadcast_to(x, (slab, HEAD_DIM))
        half = op(wide[0:slab // 2], wide[slab // 2:slab])
        half = op(half, pltpu.roll(half, HEADS_PER_GROUP, 0))
        full = jnp.concatenate([half, half], axis=0)
        return full[:, 0:x.shape[-1]]

    def per_sequence(b, carry):
        m_g, l_g, acc_g = [], [], []
        for g, (w, d) in enumerate(ATTN_GROUPS):
            k_ref, v_ref = cache_refs[g]
            kn = kn_ref[b, g]
            vn = vn_ref[b, g]
            if d % steps == 0:
                qq = qa_ref[b, g]
                s = lanesum(k_ref[b] * qq[None]) * scale
                s_new = lanesum(kn * qq) * scale
                m = jnp.maximum(jnp.max(s, axis=0), s_new)
                e = jnp.exp(s - m[None])
                e_new = jnp.exp(s_new - m)
                l_g.append(jnp.sum(e, axis=0) + e_new)
                acc_g.append(jnp.sum(e * v_ref[b], axis=0) + e_new * vn)
                m_g.append(m)
            else:
                assert d == 1 and w == k_ref.shape[1] * steps
                slabs = k_ref.shape[1]
                pos = (steps * lax.broadcasted_iota(jnp.int32, (slabs, slab, 1), 0)
                       + lax.broadcasted_iota(jnp.int32, (slabs, slab, 1), 1) // HEADS_PER_GROUP)
                m_all = jnp.zeros((slab, 1), F32)
                l_all = jnp.zeros((slab, 1), F32)
                acc_all = jnp.zeros((slab, HEAD_DIM), F32)
                for i in range(steps):
                    qq = qd_ref[b, i]
                    s = jnp.where(pos >= i, lanesum(k_ref[b] * qq[None]) * scale, MASKED)
                    s_new = jnp.where(srow // HEADS_PER_GROUP <= i, lanesum(kn * qq) * scale, MASKED)
                    m = fold(jnp.maximum(jnp.max(s, axis=0), s_new), jnp.maximum)
                    e = jnp.exp(s - m[None])
                    e_new = jnp.exp(s_new - m)
                    l = fold(jnp.sum(e, axis=0) + e_new, jnp.add)
                    acc = fold(jnp.sum(e * v_ref[b], axis=0) + e_new * vn, jnp.add)
                    mine = (srow // HEADS_PER_GROUP) == i
                    m_all = jnp.where(mine, m, m_all)
                    l_all = jnp.where(mine, l, l_all)
                    acc_all = jnp.where(mine, acc, acc_all)
                m_g.append(m_all)
                l_g.append(l_all)
                acc_g.append(acc_all)
        m = jnp.maximum(jnp.maximum(m_g[0], m_g[1]), m_g[2])
        num = jnp.zeros((slab, HEAD_DIM), F32)
        den = jnp.zeros((slab, 1), F32)
        for g in range(N_GROUPS):
            wg = jnp.exp(m_g[g] - m)
            num = num + wg * acc_g[g]
            den = den + wg * l_g[g]
        o_ref[b] = (num / den).astype(o_ref.dtype)
        return carry

    lax.fori_loop(0, bt, per_sequence, 0)


def _attn_sample(q_aligned, q_dense, k_new, v_new, caches_k, caches_v, layer):
    batch = q_aligned.shape[0]
    steps = q_dense.shape[1]
    slab = steps * HEADS_PER_GROUP
    bt = SAMPLE_ATTN_TILE
    small = lambda n: pl.BlockSpec((bt, n, slab, HEAD_DIM), lambda i: (i, 0, 0, 0))
    in_specs = [small(N_GROUPS), small(steps), small(N_GROUPS), small(N_GROUPS)]
    args = [q_aligned, q_dense, k_new, v_new]
    for g, (w, d) in enumerate(ATTN_GROUPS):
        for cache in (caches_k[g], caches_v[g]):
            depth, cb, length, heads, hd = cache.shape
            assert cb == batch and length == w and heads == HEADS_PER_GROUP and hd == HEAD_DIM
            per_slab = max(d, steps)
            view = cache.reshape(depth, batch, length // per_slab, per_slab * heads, hd)
            args.append(view)
            in_specs.append(pl.BlockSpec((None, bt, length // per_slab, slab, hd),
                                         lambda i: (layer, i, 0, 0, 0)))
    return pl.pallas_call(
        _attn_sample_kernel,
        grid=(batch // bt,),
        in_specs=in_specs,
        out_specs=pl.BlockSpec((bt, slab, HEAD_DIM), lambda i: (i, 0, 0)),
        out_shape=jax.ShapeDtypeStruct((batch, slab, HEAD_DIM), BF16),
        compiler_params=_params(("parallel",), 48),
        name="attn_sample",
    )(*args)


def _rope_tables(seq, batch, steps):
    half = ROT_DIM // 2
    inv = ROPE_THETA ** (-jnp.arange(half, dtype=F32) / half)
    pos = jnp.concatenate([jnp.arange(seq), jnp.tile(PAST_LEN + jnp.arange(steps), batch)])
    ang = pos.astype(F32)[:, None] * inv[None, :]
    cos, sin = jnp.cos(ang), jnp.sin(ang)
    rest = HEAD_DIM - ROT_DIM
    n = pos.shape[0]
    cos_t = jnp.concatenate([cos, cos, jnp.ones((n, rest), F32)], axis=1)
    sin_t = jnp.concatenate([-sin, sin, jnp.zeros((n, rest), F32)], axis=1)
    return cos_t, sin_t


def kernel(x_prompt, x_sample, cache_k_d1, cache_v_d1, cache_k_d4, cache_v_d4, cache_k_d16, cache_v_d16,
           state_pool, state_conv, ffn1_norm, ffn1_wg, ffn1_wu, ffn1_wd, mix_norm, w_in, b_gate, pool_w,
           pool_scale, conv_w, conv_b, conv_ln_g, conv_ln_b, q_norm, k_norm, proj_pool, proj_conv, proj_attn,
           w_out, ffn2_norm, ffn2_wg, ffn2_wu, ffn2_wd):
    pb, seq, dm = x_prompt.shape
    batch, steps, _ = x_sample.shape
    depth = w_in.shape[0]
    c_pool = pool_scale.shape[-1]
    c_conv = conv_w.shape[-1]
    c_attn = N_HEADS * HEAD_DIM
    grp = HEADS_PER_GROUP * HEAD_DIM
    assert pb == 1 and steps * HEADS_PER_GROUP == 16
    assert all(w == KEYS_PER_QUERY * d for w, d in ATTN_GROUPS)
    assert c_pool == c_conv
    n_prompt = pb * seq
    n_sample = batch * steps
    col_a = c_pool
    col_q = c_pool + 2 * c_conv
    col_v = col_q + 2 * c_attn
    col_gate = col_v + c_attn
    assert col_q % (2 * c_attn) == 0 and col_gate % MIX_TILE == 0

    caches_k = (cache_k_d1, cache_k_d4, cache_k_d16)
    caches_v = (cache_v_d1, cache_v_d4, cache_v_d16)
    bf = lambda t: t.astype(BF16)
    row = lambda t: t.reshape(depth, 1, t.shape[-1])
    ffn1 = (row(ffn1_norm), bf(ffn1_wg), bf(ffn1_wu), bf(ffn1_wd))
    ffn2 = (row(ffn2_norm), bf(ffn2_wg), bf(ffn2_wu), bf(ffn2_wd))
    w_in_b, pool_w_b = bf(w_in), bf(pool_w)
    pp_b, pc_b, pa_b, wo_b = bf(proj_pool), bf(proj_conv), bf(proj_attn), bf(w_out)
    mix_gain, pool_scale_r = row(mix_norm), row(pool_scale)
    conv_b_r, ln_g_r, ln_b_r = row(conv_b), row(conv_ln_g), row(conv_ln_b)
    b_gate_r = b_gate.reshape(depth, 3, dm)
    cos_t, sin_t = _rope_tables(seq, batch, steps)

    x = (x_prompt.reshape(n_prompt, dm), x_sample.reshape(n_sample, dm))
    tm_major = lambda t: jnp.transpose(t, (1, 0, 2))

    outs = {name: [] for name in ("pk", "pv", "ppool", "pconv", "sk", "sv", "spool", "sconv")}
    for l in range(depth):
        x = _ffn(x, *ffn1, l)
        z = _inproj(x, mix_gain, w_in_b, l)
        qk = _qknorm(z, jnp.stack([q_norm[l], k_norm[l]]), cos_t, sin_t, col_q // (2 * c_attn))

        pool_p = _pool_prompt(z, seq, pool_w_b, pool_scale_r, l)
        zp_s = z[n_prompt:, :c_pool].reshape(batch, steps, c_pool)
        pool_ext = jnp.concatenate([state_pool[l], zp_s], axis=1)
        pool_s = _pool_sample(tm_major(pool_ext), steps, pool_w_b, pool_scale_r, l)
        pool_ps = (pool_p, tm_major(pool_s).reshape(n_sample, c_pool))
        outs["ppool"].append(z[n_prompt - POOL_STATE:n_prompt, :c_pool].reshape(pb, POOL_STATE, c_pool))
        outs["spool"].append(pool_ext[:, -POOL_STATE:])

        conv_p, glu_tail = _conv_prompt(z, seq, col_a // c_conv, conv_w, conv_b_r, ln_g_r, ln_b_r, l)
        a_s = z[n_prompt:, col_a:col_a + c_conv].reshape(batch, steps, c_conv)
        b_s = z[n_prompt:, col_a + c_conv:col_a + 2 * c_conv].reshape(batch, steps, c_conv)
        conv_s, glu_s = _conv_sample(tm_major(a_s), tm_major(b_s), tm_major(state_conv[l]),
                                     conv_w, conv_b_r, ln_g_r, ln_b_r, l)
        conv_ps = (conv_p, tm_major(conv_s).reshape(n_sample, c_conv))
        outs["pconv"].append(glu_tail[CONV_HALO - CONV_STATE:].reshape(pb, CONV_STATE, c_conv))
        outs["sconv"].append(jnp.concatenate([state_conv[l], tm_major(glu_s)], axis=1)[:, -CONV_STATE:])

        attn_p = _attn_prompt(qk, z, seq, N_HEADS, col_v // HEAD_DIM)

        slab = steps * HEADS_PER_GROUP
        by_group = lambda t: jnp.transpose(t.reshape(batch, steps, N_GROUPS, HEADS_PER_GROUP, HEAD_DIM),
                                           (0, 2, 1, 3, 4))
        q_s = by_group(qk[n_prompt:, :c_attn])
        k_s = by_group(qk[n_prompt:, c_attn:])
        v_s = by_group(z[n_prompt:, col_v:col_v + c_attn])
        dense = [g for g, (_, d) in enumerate(ATTN_GROUPS) if d % steps != 0]
        assert dense == [0]
        q_dense = jnp.tile(q_s[:, 0], (1, 1, steps, 1))
        attn_s = _attn_sample(q_s.reshape(batch, N_GROUPS, slab, HEAD_DIM), q_dense,
                              k_s.reshape(batch, N_GROUPS, slab, HEAD_DIM),
                              v_s.reshape(batch, N_GROUPS, slab, HEAD_DIM), caches_k, caches_v, l)
        attn_ps = (attn_p, attn_s.reshape(n_sample, grp))
        for g, (w, d) in enumerate(ATTN_GROUPS):
            keep = min(w, seq)
            kg = qk[n_prompt - keep:n_prompt, c_attn + g * grp:c_attn + (g + 1) * grp]
            vg = z[n_prompt - keep:n_prompt, col_v + g * grp:col_v + (g + 1) * grp]
            outs["pk"].append(kg.reshape(pb, keep, HEADS_PER_GROUP, HEAD_DIM))
            outs["pv"].append(vg.reshape(pb, keep, HEADS_PER_GROUP, HEAD_DIM))
            outs["sk"].append(k_s[:, g])
            outs["sv"].append(v_s[:, g])

        x = _mixout(x, pool_ps, conv_ps, attn_ps, z, col_gate, b_gate_r, pp_b, pc_b, pa_b, wo_b, l)
        x = _ffn(x, *ffn2, l)

    def per_group(name, g):
        return jnp.stack([outs[name][l * N_GROUPS + g] for l in range(depth)], axis=0)

    st = lambda name: jnp.stack(outs[name], axis=0)
    return (x[:n_prompt].reshape(pb, seq, dm), x[n_prompt:].reshape(batch, steps, dm),
            per_group("pk", 0), per_group("pv", 0), per_group("pk", 1), per_group("pv", 1),
            per_group("pk", 2), per_group("pv", 2), st("ppool"), st("pconv"),
            per_group("sk", 0), per_group("sv", 0), per_group("sk", 1), per_group("sv", 1),
            per_group("sk", 2), per_group("sv", 2), st("spool"), st("sconv"))
```
